```python
import jax, jax.numpy as jnp
from jax import lax
import numpy as np

D_MODEL = 1024
BATCH = 2
SEQ = 8192
DEPTH = 4
DEC_BATCH = 128
DEC_SEQ = 1
PAST_LEN = 2048
PAGE_SIZE = 128

N_META = 16
HEAD_DIM = 64
GROUP_W = D_MODEL // 4
N_GROUP_HEADS = GROUP_W // HEAD_DIM
MIX_W = 4 * GROUP_W
N_IN_PIECES = 12
IN_COLS = N_IN_PIECES * GROUP_W
HG_CHUNK = 64
SB_BLOCK = 128
SB_SCALE = HEAD_DIM ** -0.5
SB_BIAS_INIT = -8.0
SCONV_W = 3
CCONV_W = 31
N_EXPERTS = 32
TOP_K = 4
D_EXPERT = D_MODEL
MOE_BLOCK = 256
SWIGLU_LIMIT = 7.0
SWIGLU_ALPHA = 1.702
LN_EPS = 1e-5
RMS_EPS = 1e-6
DN_ALPHA = (2 * DEPTH) ** 0.25
DN_BETA = (8 * DEPTH) ** -0.25

kernel_name = "hymba_hgrn2_stickbreak_conv_moe_step"


def layer_norm(x, g, b):
    xf = x.astype(jnp.float32)
    mu = jnp.mean(xf, axis=-1, keepdims=True)
    var = jnp.mean(jnp.square(xf - mu), axis=-1, keepdims=True)
    return ((xf - mu) * lax.rsqrt(var + LN_EPS) * g.astype(jnp.float32) + b.astype(jnp.float32)).astype(x.dtype)


def heads(t):
    return t.reshape(t.shape[:-1] + (N_GROUP_HEADS, HEAD_DIM))


def split_in(z):
    return [z[..., i * GROUP_W:(i + 1) * GROUP_W] for i in range(N_IN_PIECES)]


def causal_dwconv(u_ext, w):
    return lax.conv_general_dilated(u_ext, w[:, None, :].astype(u_ext.dtype), window_strides=(1,), padding='VALID',
                                    dimension_numbers=('NWC', 'WIO', 'NWC'), feature_group_count=u_ext.shape[-1])


def hgrn2_chunk(S, q, logf, k, v):
    L = q.shape[1]
    c = jnp.cumsum(logf, axis=1)
    causal = jnp.tril(jnp.ones((L, L), dtype=bool))[None, :, :, None, None]
    decay = jnp.exp(jnp.where(causal, c[:, :, None] - c[:, None, :], -jnp.inf))
    scores = jnp.einsum('bthk,btshk,bshk->bhts', q, decay, k)
    o = jnp.einsum('bhts,bshv->bthv', scores, v) + jnp.einsum('bthk,bhkv->bthv', q * jnp.exp(c), S)
    c_last = c[:, -1]
    S_new = jnp.exp(c_last)[..., None] * S + jnp.einsum('bshk,bshv->bhkv', k * jnp.exp(c_last[:, None] - c), v)
    return S_new, o


def hgrn2_sequence(S0, q, logf, k, v, lead):
    S, o_lead = hgrn2_chunk(S0, q[:, :lead], logf[:, :lead], k[:, :lead], v[:, :lead])
    rest = q.shape[1] - lead
    if rest == 0:
        return o_lead, S
    bn = q.shape[0]
    n = rest // HG_CHUNK

    def chunks(t):
        return jnp.moveaxis(t[:, lead:].reshape((bn, n, HG_CHUNK) + t.shape[2:]), 1, 0)

    S, o_rest = lax.scan(lambda s, xs: hgrn2_chunk(s, *xs), S, (chunks(q), chunks(logf), chunks(k), chunks(v)))
    o_rest = jnp.moveaxis(o_rest, 0, 1)
    o_rest = o_rest.reshape((bn, rest) + o_rest.shape[3:])
    return jnp.concatenate([o_lead, o_rest], axis=1), S


def sb_block(q, q_pos, k, v, k_pos, bias):
    z = jnp.einsum('bqhd,bkhd->bhqk', q.astype(jnp.float32), k.astype(jnp.float32)) * SB_SCALE \
        + bias.astype(jnp.float32)[None, :, None, None]
    valid = k_pos[None, :] < q_pos[:, None]
    log_fail = jnp.where(valid, jax.nn.log_sigmoid(-z), 0.0)
    between = lax.cumsum(log_fail, axis=3, reverse=True) - log_fail
    w = jnp.where(valid, jnp.exp(jax.nn.log_sigmoid(z) + between), 0.0)
    return jnp.einsum('bhqk,bkhd->bqhd', w, v.astype(jnp.float32))


def sb_sequence(q, k, v, q_start, lead, bias):
    bn, T, H, Dh = q.shape
    q_pos = q_start + jnp.arange(T)
    k_pos = jnp.arange(k.shape[1])
    n_lead_keys = q_start + lead
    o_lead = sb_block(q[:, :lead], q_pos[:lead], k[:, :n_lead_keys], v[:, :n_lead_keys], k_pos[:n_lead_keys], bias)
    rest = T - lead
    if rest == 0:
        return o_lead
    nb = rest // SB_BLOCK
    qb = jnp.moveaxis(q[:, lead:].reshape(bn, nb, SB_BLOCK, H, Dh), 1, 0)
    pb = q_pos[lead:].reshape(nb, SB_BLOCK)
    o_rest = lax.map(lambda a: sb_block(a[0], a[1], k, v, k_pos, bias), (qb, pb))
    o_rest = jnp.moveaxis(o_rest, 0, 1).reshape(bn, rest, H, Dh)
    return jnp.concatenate([o_lead, o_rest], axis=1)


def token_mixers(h, q_start, lead, lb, hg_S0, sb_k_past, sb_v_past, sconv_prev, cconv_prev,
                 w_in_l, w_out_l, sb_bias_l, hg_norm_w_l, sconv_w_l, cconv_w_l, cconv_b_l, cconv_ln_g_l,
                 cconv_ln_b_l):
    f32 = jnp.float32
    bn, T, _ = h.shape
    z = h @ w_in_l
    aq, af, ai, ag, bq, bk, bv, cb, cc, ch, da, dg = split_in(z)
    fl = af.astype(f32)
    lbf = lb.astype(f32)
    logf = jnp.logaddexp(jnp.log(lbf), jnp.log1p(-lbf) + jax.nn.log_sigmoid(fl))
    kA = (1.0 - lbf) * jax.nn.sigmoid(-fl)
    oA, S = hgrn2_sequence(hg_S0.astype(f32), heads(aq.astype(f32)), heads(logf), heads(kA),
                           heads(ai.astype(f32)), lead)
    oA = oA * lax.rsqrt(jnp.mean(oA * oA, axis=-1, keepdims=True) + RMS_EPS)
    oA = oA.reshape(bn, T, GROUP_W) * hg_norm_w_l.astype(f32) * jax.nn.silu(ag.astype(f32))
    kB, vB = heads(bk), heads(bv)
    if sb_k_past is None:
        k_all, v_all = kB, vB
    else:
        k_all = jnp.concatenate([sb_k_past.astype(kB.dtype), kB], axis=1)
        v_all = jnp.concatenate([sb_v_past.astype(vB.dtype), vB], axis=1)
    oB = sb_sequence(heads(bq), k_all, v_all, q_start, lead, sb_bias_l).reshape(bn, T, GROUP_W)
    u = cc * ch
    u_ext = jnp.concatenate([sconv_prev.astype(u.dtype), u], axis=1)
    oC = cb * causal_dwconv(u_ext, sconv_w_l)
    ud = da * jax.nn.sigmoid(dg)
    ud_ext = jnp.concatenate([cconv_prev.astype(ud.dtype), ud], axis=1)
    oD = jax.nn.silu(layer_norm(causal_dwconv(ud_ext, cconv_w_l) + cconv_b_l, cconv_ln_g_l, cconv_ln_b_l))
    mix = jnp.concatenate([oA.astype(h.dtype), oB.astype(h.dtype), oC.astype(h.dtype), oD.astype(h.dtype)],
                          axis=-1) @ w_out_l
    return mix, (S, kB, vB, u_ext[:, -(SCONV_W - 1):], ud_ext[:, -(CCONV_W - 1):])


def moe(h, w_router, b_router, w_gate_up, b_gate_up, w_down, b_down):
    lead_shape = h.shape[:-1]
    x = h.reshape(-1, D_MODEL)
    n_tok = x.shape[0]
    n_assign = n_tok * TOP_K
    blk = min(MOE_BLOCK, max(8, n_assign // N_EXPERTS))
    n_blocks = -(-n_assign // blk) + N_EXPERTS
    logits = jnp.dot(x, w_router).astype(jnp.float32) + b_router.astype(jnp.float32)
    top_logit, top_e = lax.top_k(logits, TOP_K)
    gates = jax.nn.softmax(top_logit, axis=-1)
    flat_e = top_e.reshape(-1).astype(jnp.int32)
    order = jnp.argsort(flat_e)
    e_sorted = flat_e[order]
    sizes = jnp.bincount(flat_e, length=N_EXPERTS).astype(jnp.int32)
    blocks_e = (sizes + blk - 1) // blk
    blk_end = jnp.cumsum(blocks_e)
    pad_start = (blk_end - blocks_e) * blk
    grp_start = jnp.cumsum(sizes) - sizes
    row = (pad_start[e_sorted] + jnp.arange(n_assign, dtype=jnp.int32) - grp_start[e_sorted]).astype(jnp.int32)
    row_tok = jnp.zeros((n_blocks * blk,), jnp.int32).at[row].set((order // TOP_K).astype(jnp.int32))
    block_e = jnp.minimum(jnp.searchsorted(blk_end, jnp.arange(n_blocks, dtype=jnp.int32), side='right'),
                          N_EXPERTS - 1)
    xb = x[row_tok].reshape(n_blocks, blk, D_MODEL)

    def expert_block(args):
        xe, e = args
        gu = xe @ w_gate_up[e] + b_gate_up[e]
        gate = jnp.minimum(gu[:, :D_EXPERT], SWIGLU_LIMIT)
        up = jnp.clip(gu[:, D_EXPERT:], -SWIGLU_LIMIT, SWIGLU_LIMIT)
        act = (up + 1.0) * gate * jax.nn.sigmoid(SWIGLU_ALPHA * gate)
        return act @ w_down[e] + b_down[e]

    yb = lax.map(expert_block, (xb, block_e)).reshape(n_blocks * blk, D_MODEL)
    assign_row = jnp.zeros((n_assign,), jnp.int32).at[order].set(row)
    y = yb[assign_row].reshape(n_tok, TOP_K, D_MODEL)
    out = jnp.einsum('tkd,tk->td', y, gates.astype(y.dtype))
    return out.reshape(lead_shape + (D_MODEL,))


def setup_inputs(seed: int = 0) -> dict:
    key = jax.random.key(seed)
    ks = jax.random.split(key, 32)
    f32 = jnp.float32

    def nrm(k, shape, s):
        return jax.random.normal(k, shape, f32) * s

    n_pages = PAST_LEN // PAGE_SIZE
    n_used = DEC_BATCH * n_pages
    n_phys = n_used + n_used // 4
    page_table = jax.random.permutation(ks[4], n_phys)[:n_used].reshape(DEC_BATCH, n_pages).astype(jnp.int32)
    return {
        "x_prompt": nrm(ks[0], (BATCH, SEQ, D_MODEL), 1.0),
        "x_sample": nrm(ks[1], (DEC_BATCH, DEC_SEQ, D_MODEL), 1.0),
        "cache_sb_k": nrm(ks[2], (n_phys, DEPTH, PAGE_SIZE, N_GROUP_HEADS, HEAD_DIM), 1.0),
        "cache_sb_v": nrm(ks[3], (n_phys, DEPTH, PAGE_SIZE, N_GROUP_HEADS, HEAD_DIM), 1.0),
        "page_table": page_table,
        "state_hgrn": nrm(ks[5], (DEC_BATCH, DEPTH, N_GROUP_HEADS, HEAD_DIM, HEAD_DIM), 0.5),
        "state_sconv": nrm(ks[6], (DEC_BATCH, DEPTH, SCONV_W - 1, GROUP_W), 1.0),
        "state_cconv": nrm(ks[7], (DEC_BATCH, DEPTH, CCONV_W - 1, GROUP_W), 1.0),
        "meta_tokens": nrm(ks[8], (N_META, D_MODEL), 1.0),
        "w_in": nrm(ks[9], (DEPTH, D_MODEL, IN_COLS), D_MODEL ** -0.5),
        "w_out": nrm(ks[10], (DEPTH, MIX_W, D_MODEL), DN_BETA * MIX_W ** -0.5),
        "sb_bias": SB_BIAS_INIT + nrm(ks[28], (DEPTH, N_GROUP_HEADS), 0.1),
        "hg_lb_logits": nrm(ks[11], (DEPTH, GROUP_W), 0.5),
        "hg_norm_w": 1.0 + nrm(ks[12], (DEPTH, GROUP_W), 0.02),
        "sconv_w": nrm(ks[13], (DEPTH, SCONV_W, GROUP_W), SCONV_W ** -0.5),
        "cconv_w": nrm(ks[14], (DEPTH, CCONV_W, GROUP_W), CCONV_W ** -0.5),
        "cconv_b": nrm(ks[15], (DEPTH, GROUP_W), 0.02),
        "cconv_ln_g": 1.0 + nrm(ks[16], (DEPTH, GROUP_W), 0.02),
        "cconv_ln_b": nrm(ks[17], (DEPTH, GROUP_W), 0.02),
        "ln1_g": 1.0 + nrm(ks[18], (DEPTH, D_MODEL), 0.02),
        "ln1_b": nrm(ks[19], (DEPTH, D_MODEL), 0.02),
        "w_router": nrm(ks[20], (DEPTH, D_MODEL, N_EXPERTS), D_MODEL ** -0.5),
        "b_router": nrm(ks[21], (DEPTH, N_EXPERTS), 0.01),
        "w_gate_up": nrm(ks[22], (DEPTH, N_EXPERTS, D_MODEL, 2 * D_EXPERT), D_MODEL ** -0.5),
        "b_gate_up": nrm(ks[23], (DEPTH, N_EXPERTS, 2 * D_EXPERT), 0.02),
        "w_down": nrm(ks[24], (DEPTH, N_EXPERTS, D_EXPERT, D_MODEL), DN_BETA * D_EXPERT ** -0.5),
        "b_down": nrm(ks[25], (DEPTH, N_EXPERTS, D_MODEL), 0.02),
        "ln2_g": 1.0 + nrm(ks[26], (DEPTH, D_MODEL), 0.02),
        "ln2_b": nrm(ks[27], (DEPTH, D_MODEL), 0.02),
    }


def reference(x_prompt, x_sample, cache_sb_k, cache_sb_v, page_table, state_hgrn, state_sconv, state_cconv,
              meta_tokens, w_in, w_out, sb_bias, hg_lb_logits, hg_norm_w, sconv_w, cconv_w, cconv_b, cconv_ln_g,
              cconv_ln_b, ln1_g, ln1_b, w_router, b_router, w_gate_up, b_gate_up, w_down, b_down, ln2_g, ln2_b):
    f32 = jnp.float32
    bp = x_prompt.shape[0]
    bs, ts = x_sample.shape[0], x_sample.shape[1]
    n_pages = page_table.shape[1]
    past = n_pages * PAGE_SIZE
    xp = jnp.concatenate([jnp.broadcast_to(meta_tokens.astype(x_prompt.dtype)[None], (bp, N_META, D_MODEL)),
                          x_prompt], axis=1)
    xs = x_sample
    lb_cum = jnp.cumsum(jax.nn.softmax(hg_lb_logits.astype(f32), axis=0), axis=0)
    lb_all = lb_cum - lb_cum[0]

    hg_p, hg_s, kp_l, vp_l, ks_l, vs_l, sc_p, sc_s, cc_p, cc_s = ([] for _ in range(10))
    for l in range(DEPTH):
        def mixers(h, q_start, lead, S0, k_past, v_past, s_prev, c_prev):
            return token_mixers(h, q_start, lead, lb_all[l], S0, k_past, v_past, s_prev, c_prev,
                                w_in[l], w_out[l], sb_bias[l], hg_norm_w[l], sconv_w[l], cconv_w[l], cconv_b[l],
                                cconv_ln_g[l], cconv_ln_b[l])

        mix_p, st_p = mixers(xp, 0, N_META, jnp.zeros((bp, N_GROUP_HEADS, HEAD_DIM, HEAD_DIM), f32), None, None,
                             jnp.zeros((bp, SCONV_W - 1, GROUP_W), xp.dtype),
                             jnp.zeros((bp, CCONV_W - 1, GROUP_W), xp.dtype))
        k_past = cache_sb_k[page_table, l].reshape(bs, past, N_GROUP_HEADS, HEAD_DIM)
        v_past = cache_sb_v[page_table, l].reshape(bs, past, N_GROUP_HEADS, HEAD_DIM)
        mix_s, st_s = mixers(xs, past, ts, state_hgrn[:, l], k_past, v_past, state_sconv[:, l], state_cconv[:, l])

        xp = layer_norm(DN_ALPHA * xp + mix_p, ln1_g[l], ln1_b[l])
        xs = layer_norm(DN_ALPHA * xs + mix_s, ln1_g[l], ln1_b[l])
        xp = layer_norm(DN_ALPHA * xp + moe(xp, w_router[l], b_router[l], w_gate_up[l], b_gate_up[l],
                                           w_down[l], b_down[l]), ln2_g[l], ln2_b[l])
        xs = layer_norm(DN_ALPHA * xs + moe(xs, w_router[l], b_router[l], w_gate_up[l], b_gate_up[l],
                                           w_down[l], b_down[l]), ln2_g[l], ln2_b[l])

        hg_p.append(st_p[0]); kp_l.append(st_p[1]); vp_l.append(st_p[2]); sc_p.append(st_p[3]); cc_p.append(st_p[4])
        hg_s.append(st_s[0]); ks_l.append(st_s[1]); vs_l.append(st_s[2]); sc_s.append(st_s[3]); cc_s.append(st_s[4])

    y_prompt = xp[:, N_META:]
    y_sample = xs
    return (y_prompt, y_sample,
            jnp.stack(hg_p, axis=1), jnp.stack(hg_s, axis=1),
            jnp.stack(kp_l, axis=1), jnp.stack(vp_l, axis=1),
            jnp.stack(ks_l, axis=1), jnp.stack(vs_l, axis=1),
            jnp.stack(sc_p, axis=1), jnp.stack(sc_s, axis=1),
            jnp.stack(cc_p, axis=1), jnp.stack(cc_s, axis=1))
```

```python
import functools

import numpy as np
import jax
import jax.numpy as jnp
from jax import lax
from jax.experimental import pallas as pl
from jax.experimental.pallas import tpu as pltpu

F32 = jnp.float32
BF16 = jnp.bfloat16

HEAD_DIM = 64
N_HEADS = 4
GROUP_W = HEAD_DIM * N_HEADS
HG_CHUNK = 64
HG_MID = HG_CHUNK // 2 - 1
ATT_TILE = 256
TOP_K = 4
MOE_BLOCK = 256
SWIGLU_LIMIT = 7.0
SWIGLU_ALPHA = 1.702
LN_EPS = 1e-5
RMS_EPS = 1e-6
CCONV_W = 31
SCONV_W = 3
CCONV_HIST = 32
SCONV_HIST = 8
SAMPLE_TILE = 16
VMEM_LIMIT = 56 * 1024 * 1024
NEG_BIG = -1e30

COL_AQ, COL_AF, COL_AI, COL_AG, COL_BQ, COL_BK, COL_BV, COL_CB, COL_CC, COL_CH, COL_DA, COL_DG = range(12)


def _dot(a, b):
    return jnp.dot(a, b, preferred_element_type=F32)


def _dot_nt(a, b):
    return lax.dot_general(a, b, (((1,), (1,)), ((), ())), preferred_element_type=F32)


def _dot_tn(a, b):
    return lax.dot_general(a, b, (((0,), (0,)), ((), ())), preferred_element_type=F32)


def _split2(x):
    hi = x.astype(BF16)
    lo = (x - hi.astype(x.dtype)).astype(BF16)
    return hi, lo


def _split3(x):
    h1 = x.astype(BF16)
    r1 = x - h1.astype(x.dtype)
    h2 = r1.astype(BF16)
    h3 = (r1 - h2.astype(x.dtype)).astype(BF16)
    return h1, h2, h3


def _dot_exact_rhs(a_bf16, x):
    h1, h2, h3 = _split3(x)
    return _dot(a_bf16, h1) + _dot(a_bf16, h2) + _dot(a_bf16, h3)


def _dot_exact_lhs(x, a_bf16):
    h1, h2, h3 = _split3(x)
    return _dot(h1, a_bf16) + _dot(h2, a_bf16) + _dot(h3, a_bf16)


def _sigmoid(x):
    return 1.0 / (1.0 + jnp.exp(-x))


def _softplus(z):
    return jnp.maximum(z, 0.0) + jnp.log1p(jnp.exp(-jnp.abs(z)))


def _layer_norm(y, g, b):
    mu = jnp.mean(y, axis=-1, keepdims=True)
    d = y - mu
    var = jnp.mean(d * d, axis=-1, keepdims=True)
    return d * lax.rsqrt(var + LN_EPS) * g + b


def _row_tile(n_rows, cap, mult):
    best = None
    for t in range(mult, cap + 1, mult):
        if n_rows % t == 0:
            best = t
    assert best is not None, (n_rows, cap, mult)
    return best


def _params(n_parallel, n_arbitrary=0):
    sem = ("parallel",) * n_parallel + ("arbitrary",) * n_arbitrary
    return pltpu.CompilerParams(dimension_semantics=sem, vmem_limit_bytes=VMEM_LIMIT)


def _inproj_kernel(x_ref, w_ref, z_ref, qkv_ref):
    z = _dot(x_ref[...], w_ref[...])
    z_ref[...] = z
    qkv_ref[...] = z[:, COL_BQ * GROUP_W:(COL_BV + 1) * GROUP_W].astype(BF16)


def _inproj(xb, w):
    n_rows, d = xb.shape
    n_cols = w.shape[1]
    tm = _row_tile(n_rows, 512, 64)
    return pl.pallas_call(
        _inproj_kernel,
        grid=(n_rows // tm,),
        in_specs=[pl.BlockSpec((tm, d), lambda i: (i, 0)),
                  pl.BlockSpec((d, n_cols), lambda i: (0, 0))],
        out_specs=[pl.BlockSpec((tm, n_cols), lambda i: (i, 0)),
                   pl.BlockSpec((tm, 3 * GROUP_W), lambda i: (i, 0))],
        out_shape=[jax.ShapeDtypeStruct((n_rows, n_cols), F32),
                   jax.ShapeDtypeStruct((n_rows, 3 * GROUP_W), BF16)],
        compiler_params=_params(1),
    )(xb, w)


def _hgrn_gates(fl, log_lb, log_1m_lb, one_m_lb):
    e = jnp.exp(-jnp.abs(fl))
    log_sig = jnp.minimum(fl, 0.0) - jnp.log1p(e)
    a = log_lb
    b = log_1m_lb + log_sig
    logf = jnp.maximum(a, b) + jnp.log1p(jnp.exp(-jnp.abs(a - b)))
    key = one_m_lb * (jnp.where(fl >= 0.0, e, 1.0) / (1.0 + e))
    return logf, key


def _hgrn_kernel(n_valid, q_ref, f_ref, i_ref, g_ref, par_ref, tri_ref, hones_ref, o_ref, st_ref, st_scr, o_scr):
    t = pl.program_id(1)
    tile = q_ref.shape[0]

    @pl.when(t == 0)
    def _():
        st_scr[...] = jnp.zeros_like(st_scr)

    logf, key = _hgrn_gates(f_ref[...], par_ref[0:1, :], par_ref[1:2, :], par_ref[2:3, :])
    row = t * tile + lax.broadcasted_iota(jnp.int32, (tile, 1), 0)
    valid = row < n_valid
    logf = jnp.where(valid, logf, 0.0)
    key = jnp.where(valid, key, 0.0)
    c = _dot_exact_rhs(tri_ref[...], logf)
    q = q_ref[...]
    v = i_ref[...]

    lane_head = lax.broadcasted_iota(jnp.int32, (1, GROUP_W), 1) // HEAD_DIM
    r_head = lax.broadcasted_iota(jnp.int32, (GROUP_W, 1), 0) // HEAD_DIM
    same_head = r_head == lane_head
    t_in = lax.broadcasted_iota(jnp.int32, (N_HEADS * HG_CHUNK, 1), 0) % HG_CHUNK
    s_in = lax.broadcasted_iota(jnp.int32, (1, HG_CHUNK), 1)
    causal = s_in <= t_in

    for j in range(tile // HG_CHUNK):
        r0 = j * HG_CHUNK
        cj = c[r0:r0 + HG_CHUNK]
        c_mid = cj[HG_MID:HG_MID + 1]
        c_last = cj[HG_CHUNK - 1:HG_CHUNK]
        qj = q[r0:r0 + HG_CHUNK]
        kj = key[r0:r0 + HG_CHUNK]
        vj = v[r0:r0 + HG_CHUNK].astype(BF16)
        q_mid = qj * jnp.exp(cj - c_mid)
        k_mid = (kj * jnp.exp(c_mid - cj)).astype(BF16)
        q_dec = (qj * jnp.exp(cj)).astype(BF16)
        k_dec = (kj * jnp.exp(c_last - cj)).astype(BF16)
        decay = jnp.exp(c_last)
        q_heads = jnp.concatenate([jnp.where(lane_head == h, q_mid, 0.0) for h in range(N_HEADS)], axis=0)
        scores = _dot_nt(q_heads.astype(BF16), k_mid)
        scores = jnp.where(causal, scores, 0.0).astype(BF16)
        o_heads = _dot(scores, vj)
        o_intra = jnp.where(lane_head == 0, o_heads[0:HG_CHUNK], 0.0)
        for h in range(1, N_HEADS):
            o_intra = o_intra + jnp.where(lane_head == h, o_heads[h * HG_CHUNK:(h + 1) * HG_CHUNK], 0.0)
        st = st_scr[...]
        o_inter = _dot_nt(q_dec, st.astype(BF16))
        o_scr[r0:r0 + HG_CHUNK, :] = o_intra + o_inter
        st_scr[...] = st * decay + jnp.where(same_head, _dot_tn(vj, k_dec), 0.0)

    o = o_scr[...]
    sq_hi, sq_lo = _split2(o * o)
    ms = (_dot(sq_hi, hones_ref[...]) + _dot(sq_lo, hones_ref[...])) * (1.0 / HEAD_DIM)
    on = o * lax.rsqrt(ms + RMS_EPS) * par_ref[3:4, :]
    g = g_ref[...]
    o_ref[...] = (on * (g * _sigmoid(g))).astype(BF16)

    @pl.when(t == pl.num_programs(1) - 1)
    def _():
        st_ref[0] = st_scr[...]


def _hgrn_prompt(z, par, n_batch, tp, n_valid):
    tile = ATT_TILE
    nt = tp // tile
    tri = np.zeros((tile, tile), np.float32)
    idx = np.arange(tile)
    tri[(idx[:, None] // HG_CHUNK == idx[None, :] // HG_CHUNK) & (idx[None, :] <= idx[:, None])] = 1.0
    hones = (idx[:, None] // HEAD_DIM == idx[None, :] // HEAD_DIM).astype(np.float32)

    def col(cb):
        return pl.BlockSpec((tile, GROUP_W), lambda b, t: (b * nt + t, cb))

    const = lambda b, t: (0, 0)
    return pl.pallas_call(
        functools.partial(_hgrn_kernel, n_valid),
        grid=(n_batch, nt),
        in_specs=[col(COL_AQ), col(COL_AF), col(COL_AI), col(COL_AG),
                  pl.BlockSpec((8, GROUP_W), const),
                  pl.BlockSpec((tile, tile), const),
                  pl.BlockSpec((GROUP_W, GROUP_W), const)],
        out_specs=[pl.BlockSpec((tile, GROUP_W), lambda b, t: (b * nt + t, 0)),
                   pl.BlockSpec((1, GROUP_W, GROUP_W), lambda b, t: (b, 0, 0))],
        out_shape=[jax.ShapeDtypeStruct((n_batch * tp, GROUP_W), BF16),
                   jax.ShapeDtypeStruct((n_batch, GROUP_W, GROUP_W), F32)],
        scratch_shapes=[pltpu.VMEM((GROUP_W, GROUP_W), F32), pltpu.VMEM((tile, GROUP_W), F32)],
        compiler_params=_params(1, 1),
    )(z, z, z, z, par, jnp.asarray(tri, BF16), jnp.asarray(hones, BF16))


def _sb_tile(z, valid, carry, cum_ref):
    sp = _softplus(z)
    if valid is not None:
        sp = jnp.where(valid, sp, 0.0)
    hi, lo = _split2(sp)
    cum = _dot(hi, cum_ref[...]) + _dot(lo, cum_ref[...])
    w = jnp.exp(z - cum - carry)
    if valid is not None:
        w = jnp.where(valid, w, 0.0)
    return w, carry + cum[:, 0:1]


def _sbattn_kernel(qi_ref, kj_ref, bias_ref, q_ref, k_ref, v_ref, cum_ref, o_ref, qm_scr, acc_scr, carry_scr):
    s = pl.program_id(1)
    qi = qi_ref[s]
    kj = kj_ref[s]
    tile = q_ref.shape[0]
    lane_head = lax.broadcasted_iota(jnp.int32, (1, GROUP_W), 1) // HEAD_DIM

    @pl.when(kj == qi)
    def _():
        acc_scr[...] = jnp.zeros_like(acc_scr)
        carry_scr[...] = jnp.zeros_like(carry_scr)
        q = q_ref[...]
        for h in range(N_HEADS):
            qm_scr[h] = jnp.where(lane_head == h, q * (HEAD_DIM ** -0.5), 0.0).astype(BF16)

    def body(diagonal):
        k = k_ref[...]
        v = v_ref[...]
        valid = None
        if diagonal:
            valid = (lax.broadcasted_iota(jnp.int32, (1, tile), 1) < lax.broadcasted_iota(jnp.int32, (tile, 1), 0))
        acc = acc_scr[...]
        for h in range(N_HEADS):
            z = _dot_nt(qm_scr[h], k) + bias_ref[h]
            w, carry = _sb_tile(z, valid, carry_scr[h], cum_ref)
            carry_scr[h] = carry
            acc = acc + jnp.where(lane_head == h, _dot(w.astype(BF16), v), 0.0)
        acc_scr[...] = acc

    pl.when(kj == qi)(lambda: body(True))
    pl.when(kj < qi)(lambda: body(False))

    @pl.when(kj == 0)
    def _():
        o_ref[...] = acc_scr[...].astype(BF16)


def _cum_matrix(n):
    idx = np.arange(n)
    return jnp.asarray((idx[:, None] >= idx[None, :]).astype(np.float32), BF16)


def _sbattn_prompt(qkv, bias, n_batch, tp):
    tile = ATT_TILE
    nq = tp // tile
    qi_list, kj_list = [], []
    for qi in range(nq):
        for kj in range(qi, -1, -1):
            qi_list.append(qi)
            kj_list.append(kj)
    qi_arr = jnp.asarray(np.array(qi_list, np.int32))
    kj_arr = jnp.asarray(np.array(kj_list, np.int32))
    grid_spec = pltpu.PrefetchScalarGridSpec(
        num_scalar_prefetch=3,
        grid=(n_batch, len(qi_list)),
        in_specs=[pl.BlockSpec((tile, GROUP_W), lambda b, s, qi, kj, bs: (b * nq + qi[s], 0)),
                  pl.BlockSpec((tile, GROUP_W), lambda b, s, qi, kj, bs: (b * nq + kj[s], 1)),
                  pl.BlockSpec((tile, GROUP_W), lambda b, s, qi, kj, bs: (b * nq + kj[s], 2)),
                  pl.BlockSpec((tile, tile), lambda b, s, qi, kj, bs: (0, 0))],
        out_specs=pl.BlockSpec((tile, GROUP_W), lambda b, s, qi, kj, bs: (b * nq + qi[s], 0)),
        scratch_shapes=[pltpu.VMEM((N_HEADS, tile, GROUP_W), BF16),
                        pltpu.VMEM((tile, GROUP_W), F32),
                        pltpu.VMEM((N_HEADS, tile, 1), F32)],
    )
    return pl.pallas_call(
        _sbattn_kernel,
        grid_spec=grid_spec,
        out_shape=jax.ShapeDtypeStruct((n_batch * tp, GROUP_W), BF16),
        compiler_params=_params(1, 1),
    )(qi_arr, kj_arr, bias, qkv, qkv, qkv, _cum_matrix(tile))


def _conv_kernel(cb_ref, cc_ref, ch_ref, da_ref, dg_ref, par_ref, cw_ref, oc_ref, od_ref, u_ref, ud_ref,
                 ubuf, dbuf):
    t = pl.program_id(1)
    tile = cb_ref.shape[0]

    @pl.when(t == 0)
    def _():
        ubuf[0:SCONV_HIST, :] = jnp.zeros((SCONV_HIST, GROUP_W), F32)
        dbuf[0:CCONV_HIST, :] = jnp.zeros((CCONV_HIST, GROUP_W), F32)

    u = cc_ref[...] * ch_ref[...]
    ud = da_ref[...] * _sigmoid(dg_ref[...])
    u_ref[...] = u
    ud_ref[...] = ud
    ubuf[SCONV_HIST:SCONV_HIST + tile, :] = u
    dbuf[CCONV_HIST:CCONV_HIST + tile, :] = ud

    conv_c = par_ref[SCONV_W - 1:SCONV_W, :] * u
    for j in range(SCONV_W - 1):
        off = SCONV_HIST - (SCONV_W - 1) + j
        conv_c = conv_c + par_ref[j:j + 1, :] * ubuf[off:off + tile, :]
    oc_ref[...] = (cb_ref[...] * conv_c).astype(BF16)

    acc = cw_ref[CCONV_W - 1:CCONV_W, :] * ud + par_ref[3:4, :]
    for j in range(CCONV_W - 1):
        off = CCONV_HIST - (CCONV_W - 1) + j
        acc = acc + cw_ref[j:j + 1, :] * dbuf[off:off + tile, :]
    y = _layer_norm(acc, par_ref[4:5, :], par_ref[5:6, :])
    od_ref[...] = (y * _sigmoid(y)).astype(BF16)

    ubuf[0:SCONV_HIST, :] = ubuf[tile:tile + SCONV_HIST, :]
    dbuf[0:CCONV_HIST, :] = dbuf[tile:tile + CCONV_HIST, :]


def _conv_prompt(z, par, cw, n_batch, tp):
    tile = ATT_TILE
    nt = tp // tile

    def col(cb):
        return pl.BlockSpec((tile, GROUP_W), lambda b, t: (b * nt + t, cb))

    const = lambda b, t: (0, 0)
    out_spec = pl.BlockSpec((tile, GROUP_W), lambda b, t: (b * nt + t, 0))
    n_rows = n_batch * tp
    return pl.pallas_call(
        _conv_kernel,
        grid=(n_batch, nt),
        in_specs=[col(COL_CB), col(COL_CC), col(COL_CH), col(COL_DA), col(COL_DG),
                  pl.BlockSpec((8, GROUP_W), const), pl.BlockSpec((CCONV_HIST, GROUP_W), const)],
        out_specs=[out_spec, out_spec, out_spec, out_spec],
        out_shape=[jax.ShapeDtypeStruct((n_rows, GROUP_W), BF16), jax.ShapeDtypeStruct((n_rows, GROUP_W), BF16),
                   jax.ShapeDtypeStruct((n_rows, GROUP_W), F32), jax.ShapeDtypeStruct((n_rows, GROUP_W), F32)],
        scratch_shapes=[pltpu.VMEM((SCONV_HIST + tile, GROUP_W), F32),
                        pltpu.VMEM((CCONV_HIST + tile, GROUP_W), F32)],
        compiler_params=_params(1, 1),
    )(z, z, z, z, z, par, cw)


def _sample_attn_kernel(n_pages, pt_ref, bias_ref, z_ref, cum_ref, *refs):
    k_refs = refs[:n_pages]
    v_refs = refs[n_pages:2 * n_pages]
    o_ref = refs[2 * n_pages]
    rows = 8
    q = z_ref[:, COL_BQ * GROUP_W:(COL_BQ + 1) * GROUP_W] * (HEAD_DIM ** -0.5)
    q8 = jnp.broadcast_to(q, (rows, GROUP_W)).astype(BF16)
    row = lax.broadcasted_iota(jnp.int32, (rows, 1), 0)
    carry = jnp.zeros((rows, 1), F32)
    acc = jnp.zeros((rows, GROUP_W), F32)
    for p in range(n_pages - 1, -1, -1):
        z = jnp.zeros((rows, k_refs[p].shape[0]), F32)
        for h in range(N_HEADS):
            kh = k_refs[p][:, h, :].astype(BF16)
            zh = _dot_nt(q8[:, h * HEAD_DIM:(h + 1) * HEAD_DIM], kh) + bias_ref[h]
            z = jnp.where(row == h, zh, z)
        w, carry = _sb_tile(z, None, carry, cum_ref)
        wb = w.astype(BF16)
        pv = [jnp.where(row == h, _dot(wb, v_refs[p][:, h, :].astype(BF16)), 0.0) for h in range(N_HEADS)]
        acc = acc + jnp.concatenate(pv, axis=1)
    o_ref[...] = jnp.sum(acc, axis=0, keepdims=True)


def _sample_attn(z3, cache_k, cache_v, page_table, bias, layer):
    n_seq = z3.shape[0]
    n_pages = page_table.shape[1]
    page = cache_k.shape[2]
    n_cols = z3.shape[2]

    def page_spec(p):
        return pl.BlockSpec((None, None, page, N_HEADS, HEAD_DIM), lambda b, pt, bs: (pt[b, p], layer, 0, 0, 0))

    grid_spec = pltpu.PrefetchScalarGridSpec(
        num_scalar_prefetch=2,
        grid=(n_seq,),
        in_specs=[pl.BlockSpec((None, 1, n_cols), lambda b, pt, bs: (b, 0, 0)),
                  pl.BlockSpec((page, page), lambda b, pt, bs: (0, 0))]
                 + [page_spec(p) for p in range(n_pages)] * 2,
        out_specs=pl.BlockSpec((None, 1, GROUP_W), lambda b, pt, bs: (b, 0, 0)),
    )
    out = pl.pallas_call(
        functools.partial(_sample_attn_kernel, n_pages),
        grid_spec=grid_spec,
        out_shape=jax.ShapeDtypeStruct((n_seq, 1, GROUP_W), F32),
        compiler_params=_params(1),
    )(page_table, bias, z3, _cum_matrix(page), *([cache_k] * n_pages), *([cache_v] * n_pages))
    return out.reshape(n_seq, GROUP_W)


def _sample_mix_kernel(z_ref, s_ref, sc_ref, cc_ref, par_ref, cpar_ref, cw_ref,
                       oa_ref, oc_ref, od_ref, sn_ref, scn_ref, ccn_ref):
    nb = z_ref.shape[0]

    def zcol(cb):
        return z_ref[:, cb * GROUP_W:(cb + 1) * GROUP_W]

    logf, key = _hgrn_gates(zcol(COL_AF), par_ref[0:1, :], par_ref[1:2, :], par_ref[2:3, :])
    f = jnp.exp(logf)
    q = zcol(COL_AQ)
    v = zcol(COL_AI)
    g = zcol(COL_AG)
    gate = par_ref[3:4, :] * (g * _sigmoid(g))
    u = zcol(COL_CC) * zcol(COL_CH)
    cb = zcol(COL_CB)
    ud = zcol(COL_DA) * _sigmoid(zcol(COL_DG))
    eye = (lax.broadcasted_iota(jnp.int32, (HEAD_DIM, HEAD_DIM), 0)
           == lax.broadcasted_iota(jnp.int32, (HEAD_DIM, HEAD_DIM), 1))

    def column(r):
        return jnp.sum(jnp.where(eye, jnp.broadcast_to(r, (HEAD_DIM, HEAD_DIM)), 0.0), axis=1, keepdims=True)

    for i in range(nb):
        o_parts = []
        for h in range(N_HEADS):
            hs = slice(h * HEAD_DIM, (h + 1) * HEAD_DIM)
            s_new = column(f[i:i + 1, hs]) * s_ref[i, h] + column(key[i:i + 1, hs]) * v[i:i + 1, hs]
            sn_ref[i, h] = s_new
            oh = jnp.sum(column(q[i:i + 1, hs]) * s_new, axis=0, keepdims=True)
            o_parts.append(oh * lax.rsqrt(jnp.mean(oh * oh, axis=-1, keepdims=True) + RMS_EPS))
        oa_ref[i:i + 1, :] = jnp.concatenate(o_parts, axis=1) * gate[i:i + 1, :]

        u_i = u[i:i + 1, :]
        conv_c = (cpar_ref[0:1, :] * sc_ref[i, 0:1, :] + cpar_ref[1:2, :] * sc_ref[i, 1:2, :]
                  + cpar_ref[2:3, :] * u_i)
        oc_ref[i:i + 1, :] = cb[i:i + 1, :] * conv_c
        scn_ref[i, 0:1, :] = sc_ref[i, 1:2, :]
        scn_ref[i, 1:2, :] = u_i

        ud_i = ud[i:i + 1, :]
        prev = cc_ref[i]
        conv_d = (jnp.sum(prev * cw_ref[0:CCONV_W - 1, :], axis=0, keepdims=True)
                  + cw_ref[CCONV_W - 1:CCONV_W, :] * ud_i + cpar_ref[3:4, :])
        y = _layer_norm(conv_d, cpar_ref[4:5, :], cpar_ref[5:6, :])
        od_ref[i:i + 1, :] = y * _sigmoid(y)
        ccn_ref[i, 0:CCONV_W - 2, :] = cc_ref[i, 1:CCONV_W - 1, :]
        ccn_ref[i, CCONV_W - 2:CCONV_W - 1, :] = ud_i


def _sample_mix(z_s, state_hgrn, state_sconv, state_cconv, par, cpar, cw, layer):
    n_seq, n_cols = z_s.shape
    nb = SAMPLE_TILE
    const = lambda i: (0, 0)
    row_spec = pl.BlockSpec((nb, GROUP_W), lambda i: (i, 0))
    return pl.pallas_call(
        _sample_mix_kernel,
        grid=(n_seq // nb,),
        in_specs=[pl.BlockSpec((nb, n_cols), lambda i: (i, 0)),
                  pl.BlockSpec((nb, None, N_HEADS, HEAD_DIM, HEAD_DIM), lambda i: (i, layer, 0, 0, 0)),
                  pl.BlockSpec((nb, None, SCONV_W - 1, GROUP_W), lambda i: (i, layer, 0, 0)),
                  pl.BlockSpec((nb, None, CCONV_W - 1, GROUP_W), lambda i: (i, layer, 0, 0)),
                  pl.BlockSpec((8, GROUP_W), const), pl.BlockSpec((8, GROUP_W), const),
                  pl.BlockSpec((CCONV_HIST, GROUP_W), const)],
        out_specs=[row_spec, row_spec, row_spec,
                   pl.BlockSpec((nb, N_HEADS, HEAD_DIM, HEAD_DIM), lambda i: (i, 0, 0, 0)),
                   pl.BlockSpec((nb, SCONV_W - 1, GROUP_W), lambda i: (i, 0, 0)),
                   pl.BlockSpec((nb, CCONV_W - 1, GROUP_W), lambda i: (i, 0, 0))],
        out_shape=[jax.ShapeDtypeStruct((n_seq, GROUP_W), F32)] * 3
                  + [jax.ShapeDtypeStruct((n_seq, N_HEADS, HEAD_DIM, HEAD_DIM), F32),
                     jax.ShapeDtypeStruct((n_seq, SCONV_W - 1, GROUP_W), F32),
                     jax.ShapeDtypeStruct((n_seq, CCONV_W - 1, GROUP_W), F32)],
        compiler_params=_params(1),
    )(z_s, state_hgrn, state_sconv, state_cconv, par, cpar, cw)


def _outproj_ln_kernel(alpha, oa_ref, ob_ref, oc_ref, od_ref, w_ref, x_ref, g_ref, b_ref, xo_ref, xb_ref):
    mix = _dot(oa_ref[...], w_ref[0:GROUP_W, :])
    for j, ref in enumerate((ob_ref, oc_ref, od_ref), start=1):
        mix = mix + _dot(ref[...], w_ref[j * GROUP_W:(j + 1) * GROUP_W, :])
    y = _layer_norm(alpha * x_ref[...] + mix, g_ref[...], b_ref[...])
    xo_ref[...] = y
    xb_ref[...] = y.astype(BF16)


def _outproj_ln(oa, ob, oc, od, w, x, g, b, alpha):
    n_rows, d = x.shape
    tm = _row_tile(n_rows, 1024, 128)
    part = pl.BlockSpec((tm, GROUP_W), lambda i: (i, 0))
    full = pl.BlockSpec((tm, d), lambda i: (i, 0))
    vec = pl.BlockSpec((1, d), lambda i: (0, 0))
    return pl.pallas_call(
        functools.partial(_outproj_ln_kernel, alpha),
        grid=(n_rows // tm,),
        in_specs=[part, part, part, part, pl.BlockSpec((4 * GROUP_W, d), lambda i: (0, 0)), full, vec, vec],
        out_specs=[full, full],
        out_shape=[jax.ShapeDtypeStruct((n_rows, d), F32), jax.ShapeDtypeStruct((n_rows, d), BF16)],
        compiler_params=_params(1),
    )(oa, ob, oc, od, w, x, g, b)


def _router_kernel(x_ref, w1_ref, w2_ref, w3_ref, b_ref, e_ref, g_ref):
    x1, x2, x3 = _split3(x_ref[...])
    w1, w2, w3 = w1_ref[...], w2_ref[...], w3_ref[...]
    logits = (_dot(x1, w1) + (_dot(x1, w2) + _dot(x2, w1)) + (_dot(x1, w3) + _dot(x2, w2) + _dot(x3, w1))
              + b_ref[...])
    n_lanes = logits.shape[1]
    lane = lax.broadcasted_iota(jnp.int32, (1, n_lanes), 1)
    e_out = jnp.zeros(logits.shape, jnp.int32)
    g_out = jnp.zeros(logits.shape, F32)
    denom = jnp.zeros((logits.shape[0], 1), F32)
    top = None
    for k in range(TOP_K):
        m = jnp.max(logits, axis=-1, keepdims=True)
        idx = jnp.min(jnp.where(logits == m, lane, n_lanes), axis=-1, keepdims=True)
        if top is None:
            top = m
        p = jnp.exp(m - top)
        denom = denom + p
        e_out = jnp.where(lane == k, idx, e_out)
        g_out = jnp.where(lane == k, p, g_out)
        logits = jnp.where(lane == idx, NEG_BIG * 2.0, logits)
    e_ref[...] = e_out
    g_ref[...] = g_out / denom


def _router(x, w_router, b_router):
    n_rows, d = x.shape
    n_exp = w_router.shape[1]
    lanes = 128
    wp = jnp.zeros((d, lanes), F32).at[:, :n_exp].set(w_router)
    bp = jnp.full((1, lanes), NEG_BIG, F32).at[0, :n_exp].set(b_router)
    w1, w2, w3 = _split3(wp)
    tm = _row_tile(n_rows, 1024, 128)
    wspec = pl.BlockSpec((d, lanes), lambda i: (0, 0))
    ospec = pl.BlockSpec((tm, lanes), lambda i: (i, 0))
    return pl.pallas_call(
        _router_kernel,
        grid=(n_rows // tm,),
        in_specs=[pl.BlockSpec((tm, d), lambda i: (i, 0)), wspec, wspec, wspec,
                  pl.BlockSpec((1, lanes), lambda i: (0, 0))],
        out_specs=[ospec, ospec],
        out_shape=[jax.ShapeDtypeStruct((n_rows, lanes), jnp.int32), jax.ShapeDtypeStruct((n_rows, lanes), F32)],
        compiler_params=_params(1),
    )(x, w1, w2, w3, bp)


def _expert_kernel(be_ref, nu_ref, idx_ref, x_hbm, wgu_ref, bgu_ref, wd_ref, bd_ref, y_hbm,
                   idx_smem, xbuf, ybuf, wgu_bf, wd_bf, sem_idx, sem_g, sem_s):
    i = pl.program_id(0)
    n = pl.num_programs(0)
    slot = i % 2
    other = 1 - slot
    blk = xbuf.shape[1]
    d_exp = wd_ref.shape[0]

    def idx_copy(block, s):
        return pltpu.make_async_copy(idx_ref.at[block], idx_smem.at[s], sem_idx.at[s])

    def gather_copy(s, j, r):
        return pltpu.make_async_copy(x_hbm.at[pl.ds(r, 1)], xbuf.at[s, pl.ds(j, 1)], sem_g.at[s])

    def scatter_copy(s, j, r):
        return pltpu.make_async_copy(ybuf.at[s, pl.ds(j, 1)], y_hbm.at[pl.ds(r, 1)], sem_s.at[s])

    def for_rows(fn):
        def step(j, carry):
            fn(j)
            return carry
        lax.fori_loop(0, blk, step, 0, unroll=8)

    def start_gather(s, q):
        for_rows(lambda j: gather_copy(s, j, idx_smem[q, j]).start())

    def wait_gather(s):
        for_rows(lambda j: gather_copy(s, j, 0).wait())

    def start_scatter(s, q):
        for_rows(lambda j: scatter_copy(s, j, idx_smem[q, blk + j]).start())

    def wait_scatter(s):
        for_rows(lambda j: scatter_copy(s, j, 0).wait())

    @pl.when(i == 0)
    def _():
        idx_copy(0, 0).start()
        idx_copy(0, 0).wait()

        @pl.when(n > 1)
        def _():
            idx_copy(1, 1).start()

        start_gather(0, 0)

    @pl.when(i + 2 < n)
    def _():
        idx_copy(i + 2, (i + 2) % 3).start()

    wait_gather(slot)

    @pl.when(i + 1 < n)
    def _():
        idx_copy(i + 1, (i + 1) % 3).wait()
        start_gather(other, (i + 1) % 3)

    @pl.when(i >= 2)
    def _():
        wait_scatter(slot)

    changed = jnp.logical_or(i == 0, be_ref[i] != be_ref[jnp.maximum(i - 1, 0)])

    @pl.when(changed)
    def _():
        wgu_bf[...] = wgu_ref[...].astype(BF16)
        wd_bf[...] = wd_ref[...].astype(BF16)

    @pl.when(i < nu_ref[0])
    def _():
        xb = xbuf[slot].astype(BF16)
        gu = _dot(xb, wgu_bf[...]) + bgu_ref[...]
        gate = jnp.minimum(gu[:, :d_exp], SWIGLU_LIMIT)
        up = jnp.clip(gu[:, d_exp:], -SWIGLU_LIMIT, SWIGLU_LIMIT)
        act = (up + 1.0) * gate * _sigmoid(SWIGLU_ALPHA * gate)
        ybuf[slot] = _dot(act.astype(BF16), wd_bf[...]) + bd_ref[...]

    @pl.when(i >= nu_ref[0])
    def _():
        ybuf[slot] = jnp.zeros(ybuf.shape[1:], F32)

    start_scatter(slot, i % 3)

    @pl.when(i == n - 1)
    def _():
        wait_scatter(slot)

        @pl.when(n >= 2)
        def _():
            wait_scatter(other)


def _experts(x, idx, block_e, n_used, w_gate_up, b_gate_up, w_down, b_down, layer, n_slots):
    n_blocks = idx.shape[0]
    blk = idx.shape[1] // 2
    d = x.shape[1]
    d2 = w_gate_up.shape[3]
    grid_spec = pltpu.PrefetchScalarGridSpec(
        num_scalar_prefetch=2,
        grid=(n_blocks,),
        in_specs=[pl.BlockSpec(idx.shape, lambda i, be, nu: (0, 0)),
                  pl.BlockSpec(memory_space=pl.ANY),
                  pl.BlockSpec((None, None, d, d2), lambda i, be, nu: (layer, be[i], 0, 0)),
                  pl.BlockSpec((None, None, 1, d2), lambda i, be, nu: (layer, be[i], 0, 0)),
                  pl.BlockSpec((None, None, d2 // 2, d), lambda i, be, nu: (layer, be[i], 0, 0)),
                  pl.BlockSpec((None, None, 1, d), lambda i, be, nu: (layer, be[i], 0, 0))],
        out_specs=pl.BlockSpec(memory_space=pl.ANY),
        scratch_shapes=[pltpu.SMEM((3, 2 * blk), jnp.int32),
                        pltpu.VMEM((2, blk, d), F32),
                        pltpu.VMEM((2, blk, d), F32),
                        pltpu.VMEM((d, d2), BF16),
                        pltpu.VMEM((d2 // 2, d), BF16),
                        pltpu.SemaphoreType.DMA((3,)),
                        pltpu.SemaphoreType.DMA((2,)),
                        pltpu.SemaphoreType.DMA((2,))],
    )
    return pl.pallas_call(
        _expert_kernel,
        grid_spec=grid_spec,
        out_shape=jax.ShapeDtypeStruct((n_slots, d), F32),
        compiler_params=_params(0, 1),
    )(block_e, n_used, idx, x, w_gate_up, b_gate_up, w_down, b_down)


def _route(top_e, n_rows, n_experts):
    blk = MOE_BLOCK
    n_assign = n_rows * TOP_K
    n_blocks = -(-n_assign // blk) + n_experts
    flat_e = top_e.reshape(-1)
    order = jnp.argsort(flat_e).astype(jnp.int32)
    e_sorted = flat_e[order]
    sizes = jnp.bincount(flat_e, length=n_experts).astype(jnp.int32)
    blocks_e = (sizes + blk - 1) // blk
    blk_end = jnp.cumsum(blocks_e)
    pad_start = (blk_end - blocks_e) * blk
    grp_start = jnp.cumsum(sizes) - sizes
    slot = pad_start[e_sorted] + jnp.arange(n_assign, dtype=jnp.int32) - grp_start[e_sorted]
    tok = order // TOP_K
    src = jnp.zeros((n_blocks * blk,), jnp.int32).at[slot].set(tok)
    pos = jnp.arange(n_blocks * blk, dtype=jnp.int32)
    dump = n_assign + ((pos // blk) % 2) * blk + pos % blk
    dst = dump.at[slot].set((order % TOP_K) * n_rows + tok)
    idx = jnp.concatenate([src.reshape(n_blocks, blk), dst.reshape(n_blocks, blk)], axis=1)
    block_e = jnp.minimum(jnp.searchsorted(blk_end, jnp.arange(n_blocks, dtype=jnp.int32), side='right'),
                          n_experts - 1).astype(jnp.int32)
    n_used = blk_end[-1:].astype(jnp.int32)
    return idx, block_e, n_used, n_assign + 2 * blk


def _combine_ln_kernel(alpha, x_ref, y0_ref, y1_ref, y2_ref, y3_ref, gate_ref, g_ref, b_ref, xo_ref, xb_ref):
    gates = gate_ref[...]
    mix = gates[:, 0:1] * y0_ref[...]
    for k, ref in enumerate((y1_ref, y2_ref, y3_ref), start=1):
        mix = mix + gates[:, k:k + 1] * ref[...]
    y = _layer_norm(alpha * x_ref[...] + mix, g_ref[...], b_ref[...])
    xo_ref[...] = y
    xb_ref[...] = y.astype(BF16)


def _combine_ln(x, y_slots, gates, g, b, alpha):
    n_rows, d = x.shape
    tm = _row_tile(n_rows, 512, 64)
    nt = n_rows // tm
    full = pl.BlockSpec((tm, d), lambda i: (i, 0))
    vec = pl.BlockSpec((1, d), lambda i: (0, 0))

    def yspec(k):
        return pl.BlockSpec((tm, d), lambda i: (k * nt + i, 0))

    return pl.pallas_call(
        functools.partial(_combine_ln_kernel, alpha),
        grid=(nt,),
        in_specs=[full, yspec(0), yspec(1), yspec(2), yspec(3),
                  pl.BlockSpec((tm, gates.shape[1]), lambda i: (i, 0)), vec, vec],
        out_specs=[full, full],
        out_shape=[jax.ShapeDtypeStruct((n_rows, d), F32), jax.ShapeDtypeStruct((n_rows, d), BF16)],
        compiler_params=_params(1),
    )(x, y_slots, y_slots, y_slots, y_slots, gates, g, b)


def _pad_rows(a, n):
    return jnp.zeros((n, a.shape[1]), a.dtype).at[:a.shape[0]].set(a)


def kernel(x_prompt, x_sample, cache_sb_k, cache_sb_v, page_table, state_hgrn, state_sconv, state_cconv, meta_tokens, w_in, w_out, sb_bias, hg_lb_logits, hg_norm_w, sconv_w, cconv_w, cconv_b, cconv_ln_g, cconv_ln_b, ln1_g, ln1_b, w_router, b_router, w_gate_up, b_gate_up, w_down, b_down, ln2_g, ln2_b):
    n_batch, seq, d = x_prompt.shape
    n_seq = x_sample.shape[0]
    depth = w_in.shape[0]
    n_meta = meta_tokens.shape[0]
    n_experts = w_router.shape[2]
    t_len = n_meta + seq
    tp = -(-t_len // ATT_TILE) * ATT_TILE
    n_prompt_rows = n_batch * tp
    n_rows = n_prompt_rows + n_seq
    alpha = float((2 * depth) ** 0.25)
    assert d == 4 * GROUP_W and x_sample.shape[1] == 1 and n_seq % SAMPLE_TILE == 0

    pieces = []
    for b in range(n_batch):
        pieces += [meta_tokens.astype(F32), x_prompt[b], jnp.zeros((tp - t_len, d), F32)]
    pieces.append(x_sample.reshape(n_seq, d))
    x = jnp.concatenate(pieces, axis=0)
    xb = x.astype(BF16)

    lb_cum = jnp.cumsum(jax.nn.softmax(hg_lb_logits.astype(F32), axis=0), axis=0)
    lb_all = lb_cum - lb_cum[0]

    hg_p, hg_s, kp_l, vp_l, ks_l, vs_l, sc_p, sc_s, cc_p, cc_s = ([] for _ in range(10))
    for l in range(depth):
        lb = lb_all[l]
        hpar = _pad_rows(jnp.stack([jnp.log(lb), jnp.log1p(-lb), 1.0 - lb, hg_norm_w[l]]), 8)
        cpar = _pad_rows(jnp.concatenate([sconv_w[l], cconv_b[l][None], cconv_ln_g[l][None],
                                          cconv_ln_b[l][None]], axis=0), 8)
        cw = _pad_rows(cconv_w[l], CCONV_HIST)

        z, qkv = _inproj(xb, w_in[l].astype(BF16))

        oa_p, st = _hgrn_prompt(z, hpar, n_batch, tp, t_len)
        ob_p = _sbattn_prompt(qkv, sb_bias[l], n_batch, tp)
        oc_p, od_p, u_p, ud_p = _conv_prompt(z, cpar, cw, n_batch, tp)

        z_s = z[n_prompt_rows:]
        ob_s = _sample_attn(z_s.reshape(n_seq, 1, z.shape[1]), cache_sb_k, cache_sb_v, page_table, sb_bias[l], l)
        oa_s, oc_s, od_s, s_new, sc_new, cc_new = _sample_mix(z_s, state_hgrn, state_sconv, state_cconv,
                                                              hpar, cpar, cw, l)

        oa = jnp.concatenate([oa_p, oa_s.astype(BF16)], axis=0)
        ob = jnp.concatenate([ob_p, ob_s.astype(BF16)], axis=0)
        oc = jnp.concatenate([oc_p, oc_s.astype(BF16)], axis=0)
        od = jnp.concatenate([od_p, od_s.astype(BF16)], axis=0)
        x, xb = _outproj_ln(oa, ob, oc, od, w_out[l].astype(BF16), x, ln1_g[l][None], ln1_b[l][None], alpha)

        top_e, gates = _router(x, w_router[l], b_router[l])
        idx, block_e, n_used, n_slots = _route(top_e[:, :TOP_K], n_rows, n_experts)
        y_slots = _experts(x, idx, block_e, n_used, w_gate_up, b_gate_up.reshape(depth, n_experts, 1, -1),
                           w_down, b_down.reshape(depth, n_experts, 1, -1), l, n_slots)
        x, xb = _combine_ln(x, y_slots, gates, ln2_g[l][None], ln2_b[l][None], alpha)

        zp = z[:n_prompt_rows].reshape(n_batch, tp, -1)
        st4 = st.reshape(n_batch, N_HEADS, HEAD_DIM, N_HEADS, HEAD_DIM)
        hg_p.append(jnp.stack([st4[:, h, :, h, :] for h in range(N_HEADS)], axis=1).swapaxes(-1, -2))
        hg_s.append(s_new)
        kp_l.append(zp[:, :t_len, COL_BK * GROUP_W:(COL_BK + 1) * GROUP_W].reshape(n_batch, t_len, N_HEADS, HEAD_DIM))
        vp_l.append(zp[:, :t_len, COL_BV * GROUP_W:(COL_BV + 1) * GROUP_W].reshape(n_batch, t_len, N_HEADS, HEAD_DIM))
        ks_l.append(z_s[:, COL_BK * GROUP_W:(COL_BK + 1) * GROUP_W].reshape(n_seq, 1, N_HEADS, HEAD_DIM))
        vs_l.append(z_s[:, COL_BV * GROUP_W:(COL_BV + 1) * GROUP_W].reshape(n_seq, 1, N_HEADS, HEAD_DIM))
        sc_p.append(u_p.reshape(n_batch, tp, GROUP_W)[:, t_len - (SCONV_W - 1):t_len])
        cc_p.append(ud_p.reshape(n_batch, tp, GROUP_W)[:, t_len - (CCONV_W - 1):t_len])
        sc_s.append(sc_new)
        cc_s.append(cc_new)

    y_prompt = x[:n_prompt_rows].reshape(n_batch, tp, d)[:, n_meta:t_len]
    y_sample = x[n_prompt_rows:].reshape(n_seq, 1, d)
    return (y_prompt, y_sample,
            jnp.stack(hg_p, axis=1), jnp.stack(hg_s, axis=1),
            jnp.stack(kp_l, axis=1), jnp.stack(vp_l, axis=1),
            jnp.stack(ks_l, axis=1), jnp.stack(vs_l, axis=1),
            jnp.stack(sc_p, axis=1), jnp.stack(sc_s, axis=1),
            jnp.stack(cc_p, axis=1), jnp.stack(cc_s, axis=1))
```

```python
import functools

import numpy as np
import jax
import jax.numpy as jnp
from jax import lax
from jax.experimental import pallas as pl
from jax.experimental.pallas import tpu as pltpu

F32 = jnp.float32
BF16 = jnp.bfloat16

HEAD_DIM = 64
N_HEADS = 4
GROUP_W = HEAD_DIM * N_HEADS
HG_CHUNK = 64
HG_MID = HG_CHUNK // 2 - 1
ATT_TILE = 256
TOP_K = 4
MOE_BLOCK = 256
SWIGLU_LIMIT = 7.0
SWIGLU_ALPHA = 1.702
LN_EPS = 1e-5
RMS_EPS = 1e-6
CCONV_W = 31
SCONV_W = 3
CCONV_HIST = 32
SCONV_HIST = 8
SAMPLE_TILE = 16
VMEM_LIMIT = 56 * 1024 * 1024
NEG_BIG = -1e30
LOG2E = 1.4426950408889634

COL_AQ, COL_AF, COL_AI, COL_AG, COL_BQ, COL_BK, COL_BV, COL_CB, COL_CC, COL_CH, COL_DA, COL_DG = range(12)


def _dot(a, b):
    return jnp.dot(a, b, preferred_element_type=F32)


def _dot_nt(a, b):
    return lax.dot_general(a, b, (((1,), (1,)), ((), ())), preferred_element_type=F32)


def _dot_tn(a, b):
    return lax.dot_general(a, b, (((0,), (0,)), ((), ())), preferred_element_type=F32)


def _split2(x):
    hi = x.astype(BF16)
    lo = (x - hi.astype(x.dtype)).astype(BF16)
    return hi, lo


def _split3(x):
    h1 = x.astype(BF16)
    r1 = x - h1.astype(x.dtype)
    h2 = r1.astype(BF16)
    h3 = (r1 - h2.astype(x.dtype)).astype(BF16)
    return h1, h2, h3


def _dot_exact_rhs(a_bf16, x):
    h1, h2, h3 = _split3(x)
    return _dot(a_bf16, h1) + _dot(a_bf16, h2) + _dot(a_bf16, h3)


def _dot_exact_lhs(x, a_bf16):
    h1, h2, h3 = _split3(x)
    return _dot(h1, a_bf16) + _dot(h2, a_bf16) + _dot(h3, a_bf16)


def _sigmoid(x):
    return 1.0 / (1.0 + jnp.exp(-x))


def _softplus(z):
    return jnp.maximum(z, 0.0) + jnp.log(1.0 + jnp.exp(-jnp.abs(z)))


def _layer_norm(y, g, b):
    mu = jnp.mean(y, axis=-1, keepdims=True)
    d = y - mu
    var = jnp.mean(d * d, axis=-1, keepdims=True)
    return d * lax.rsqrt(var + LN_EPS) * g + b


def _row_tile(n_rows, cap, mult):
    best = None
    for t in range(mult, cap + 1, mult):
        if n_rows % t == 0:
            best = t
    assert best is not None, (n_rows, cap, mult)
    return best


def _params(n_parallel, n_arbitrary=0):
    sem = ("parallel",) * n_parallel + ("arbitrary",) * n_arbitrary
    return pltpu.CompilerParams(dimension_semantics=sem, vmem_limit_bytes=VMEM_LIMIT)


def _inproj_kernel(x_ref, w_ref, z_ref, qkv_ref):
    z = _dot(x_ref[...], w_ref[...])
    z_ref[...] = z
    qkv_ref[...] = z[:, COL_BQ * GROUP_W:(COL_BV + 1) * GROUP_W].astype(BF16)


def _inproj(xb, w):
    n_rows, d = xb.shape
    n_cols = w.shape[1]
    tm = _row_tile(n_rows, 512, 64)
    return pl.pallas_call(
        _inproj_kernel,
        grid=(n_rows // tm,),
        in_specs=[pl.BlockSpec((tm, d), lambda i: (i, 0)),
                  pl.BlockSpec((d, n_cols), lambda i: (0, 0))],
        out_specs=[pl.BlockSpec((tm, n_cols), lambda i: (i, 0)),
                   pl.BlockSpec((tm, 3 * GROUP_W), lambda i: (i, 0))],
        out_shape=[jax.ShapeDtypeStruct((n_rows, n_cols), F32),
                   jax.ShapeDtypeStruct((n_rows, 3 * GROUP_W), BF16)],
        compiler_params=_params(1),
    )(xb, w)


def _hgrn_gates(fl, log_lb, log_1m_lb, one_m_lb):
    e = jnp.exp(-jnp.abs(fl))
    log_sig = jnp.minimum(fl, 0.0) - jnp.log1p(e)
    a = log_lb
    b = log_1m_lb + log_sig
    logf = jnp.maximum(a, b) + jnp.log1p(jnp.exp(-jnp.abs(a - b)))
    key = one_m_lb * (jnp.where(fl >= 0.0, e, 1.0) / (1.0 + e))
    return logf, key


def _hgrn_kernel(n_valid, q_ref, f_ref, i_ref, g_ref, par_ref, tri_ref, hones_ref, o_ref, st_ref, st_scr, o_scr):
    t = pl.program_id(1)
    tile = q_ref.shape[0]

    @pl.when(t == 0)
    def _():
        st_scr[...] = jnp.zeros_like(st_scr)

    logf, key = _hgrn_gates(f_ref[...], par_ref[0:1, :], par_ref[1:2, :], par_ref[2:3, :])
    row = t * tile + lax.broadcasted_iota(jnp.int32, (tile, 1), 0)
    valid = row < n_valid
    logf = jnp.where(valid, logf, 0.0)
    key = jnp.where(valid, key, 0.0)
    c = _dot_exact_rhs(tri_ref[...], logf)
    q = q_ref[...]
    v = i_ref[...]

    lane_head = lax.broadcasted_iota(jnp.int32, (1, GROUP_W), 1) // HEAD_DIM
    r_head = lax.broadcasted_iota(jnp.int32, (GROUP_W, 1), 0) // HEAD_DIM
    same_head = r_head == lane_head
    t_in = lax.broadcasted_iota(jnp.int32, (N_HEADS * HG_CHUNK, 1), 0) % HG_CHUNK
    s_in = lax.broadcasted_iota(jnp.int32, (1, HG_CHUNK), 1)
    causal = s_in <= t_in

    for j in range(tile // HG_CHUNK):
        r0 = j * HG_CHUNK
        cj = c[r0:r0 + HG_CHUNK]
        c_mid = cj[HG_MID:HG_MID + 1]
        c_last = cj[HG_CHUNK - 1:HG_CHUNK]
        qj = q[r0:r0 + HG_CHUNK]
        kj = key[r0:r0 + HG_CHUNK]
        vj = v[r0:r0 + HG_CHUNK].astype(BF16)
        q_mid = qj * jnp.exp(cj - c_mid)
        k_mid = (kj * jnp.exp(c_mid - cj)).astype(BF16)
        q_dec = (qj * jnp.exp(cj)).astype(BF16)
        k_dec = (kj * jnp.exp(c_last - cj)).astype(BF16)
        decay = jnp.exp(c_last)
        q_heads = jnp.concatenate([jnp.where(lane_head == h, q_mid, 0.0) for h in range(N_HEADS)], axis=0)
        scores = _dot_nt(q_heads.astype(BF16), k_mid)
        scores = jnp.where(causal, scores, 0.0).astype(BF16)
        o_heads = _dot(scores, vj)
        o_intra = jnp.where(lane_head == 0, o_heads[0:HG_CHUNK], 0.0)
        for h in range(1, N_HEADS):
            o_intra = o_intra + jnp.where(lane_head == h, o_heads[h * HG_CHUNK:(h + 1) * HG_CHUNK], 0.0)
        st = st_scr[...]
        o_inter = _dot_nt(q_dec, st.astype(BF16))
        o_scr[r0:r0 + HG_CHUNK, :] = o_intra + o_inter
        st_scr[...] = st * decay + jnp.where(same_head, _dot_tn(vj, k_dec), 0.0)

    o = o_scr[...]
    sq_hi, sq_lo = _split2(o * o)
    ms = (_dot(sq_hi, hones_ref[...]) + _dot(sq_lo, hones_ref[...])) * (1.0 / HEAD_DIM)
    on = o * lax.rsqrt(ms + RMS_EPS) * par_ref[3:4, :]
    g = g_ref[...]
    o_ref[...] = (on * (g * _sigmoid(g))).astype(BF16)

    @pl.when(t == pl.num_programs(1) - 1)
    def _():
        st_ref[0] = st_scr[...]


def _hgrn_prompt(z, par, n_batch, tp, n_valid):
    tile = ATT_TILE
    nt = tp // tile
    tri = np.zeros((tile, tile), np.float32)
    idx = np.arange(tile)
    tri[(idx[:, None] // HG_CHUNK == idx[None, :] // HG_CHUNK) & (idx[None, :] <= idx[:, None])] = 1.0
    hones = (idx[:, None] // HEAD_DIM == idx[None, :] // HEAD_DIM).astype(np.float32)

    def col(cb):
        return pl.BlockSpec((tile, GROUP_W), lambda b, t: (b * nt + t, cb))

    const = lambda b, t: (0, 0)
    return pl.pallas_call(
        functools.partial(_hgrn_kernel, n_valid),
        grid=(n_batch, nt),
        in_specs=[col(COL_AQ), col(COL_AF), col(COL_AI), col(COL_AG),
                  pl.BlockSpec((8, GROUP_W), const),
                  pl.BlockSpec((tile, tile), const),
                  pl.BlockSpec((GROUP_W, GROUP_W), const)],
        out_specs=[pl.BlockSpec((tile, GROUP_W), lambda b, t: (b * nt + t, 0)),
                   pl.BlockSpec((1, GROUP_W, GROUP_W), lambda b, t: (b, 0, 0))],
        out_shape=[jax.ShapeDtypeStruct((n_batch * tp, GROUP_W), BF16),
                   jax.ShapeDtypeStruct((n_batch, GROUP_W, GROUP_W), F32)],
        scratch_shapes=[pltpu.VMEM((GROUP_W, GROUP_W), F32), pltpu.VMEM((tile, GROUP_W), F32)],
        compiler_params=_params(1, 1),
    )(z, z, z, z, par, jnp.asarray(tri, BF16), jnp.asarray(hones, BF16))


def _sb_tile(z2, valid, carry, cum_ref):
    neg_abs = lax.bitcast_convert_type(lax.bitcast_convert_type(z2, jnp.uint32) | jnp.uint32(0x80000000), F32)
    sp = jnp.maximum(z2, 0.0) + jnp.log2(1.0 + jnp.exp2(neg_abs))
    if valid is not None:
        sp = jnp.where(valid, sp, 0.0)
    hi, lo = _split2(sp)
    cum = _dot(hi, cum_ref[...]) + _dot(lo, cum_ref[...])
    w = jnp.exp2(z2 - cum - carry)
    if valid is not None:
        w = jnp.where(valid, w, 0.0)
    return w, carry + cum[:, 0:1]


def _sbattn_kernel(qi_ref, kj_ref, bias_ref, q_ref, k_ref, v_ref, cum_ref, o_ref,
                   qm_scr, acc_scr, carry_scr):
    s = pl.program_id(1)
    qi = qi_ref[s]
    kj = kj_ref[s]
    tile = q_ref.shape[0]
    lane_head = lax.broadcasted_iota(jnp.int32, (1, GROUP_W), 1) // HEAD_DIM

    @pl.when(kj == qi)
    def _():
        acc_scr[...] = jnp.zeros_like(acc_scr)
        carry_scr[...] = jnp.zeros_like(carry_scr)
        q = q_ref[...]
        for h in range(N_HEADS):
            qm_scr[h] = jnp.where(lane_head == h, q * (LOG2E * HEAD_DIM ** -0.5), 0.0).astype(BF16)

    def body(diagonal):
        k = k_ref[...]
        v = v_ref[...]
        valid = None
        if diagonal:
            valid = (lax.broadcasted_iota(jnp.int32, (1, tile), 1) < lax.broadcasted_iota(jnp.int32, (tile, 1), 0))
        pv = None
        for h in range(N_HEADS):
            z2 = _dot_nt(qm_scr[h], k) + bias_ref[h] * LOG2E
            w, carry = _sb_tile(z2, valid, carry_scr[h], cum_ref)
            carry_scr[h] = carry
            pv_h = _dot(w.astype(BF16), jnp.where(lane_head == h, v, jnp.zeros_like(v)))
            pv = pv_h if pv is None else pv + pv_h
        acc_scr[...] += pv

    pl.when(kj == qi)(lambda: body(True))
    pl.when(kj < qi)(lambda: body(False))

    @pl.when(kj == 0)
    def _():
        o_ref[...] = acc_scr[...].astype(BF16)


def _cum_matrix(n):
    idx = np.arange(n)
    return jnp.asarray((idx[:, None] >= idx[None, :]).astype(np.float32), BF16)


def _sbattn_prompt(qkv, bias, n_batch, tp):
    tile = ATT_TILE
    nq = tp // tile
    qi_list, kj_list = [], []
    for qi in range(nq):
        for kj in range(qi, -1, -1):
            qi_list.append(qi)
            kj_list.append(kj)
    qi_arr = jnp.asarray(np.array(qi_list, np.int32))
    kj_arr = jnp.asarray(np.array(kj_list, np.int32))
    grid_spec = pltpu.PrefetchScalarGridSpec(
        num_scalar_prefetch=3,
        grid=(n_batch, len(qi_list)),
        in_specs=[pl.BlockSpec((tile, GROUP_W), lambda b, s, qi, kj, bs: (b * nq + qi[s], 0)),
                  pl.BlockSpec((tile, GROUP_W), lambda b, s, qi, kj, bs: (b * nq + kj[s], 1)),
                  pl.BlockSpec((tile, GROUP_W), lambda b, s, qi, kj, bs: (b * nq + kj[s], 2)),
                  pl.BlockSpec((tile, tile), lambda b, s, qi, kj, bs: (0, 0))],
        out_specs=pl.BlockSpec((tile, GROUP_W), lambda b, s, qi, kj, bs: (b * nq + qi[s], 0)),
        scratch_shapes=[pltpu.VMEM((N_HEADS, tile, GROUP_W), BF16),
                        pltpu.VMEM((tile, GROUP_W), F32),
                        pltpu.VMEM((N_HEADS, tile, 1), F32)],
    )
    return pl.pallas_call(
        _sbattn_kernel,
        grid_spec=grid_spec,
        out_shape=jax.ShapeDtypeStruct((n_batch * tp, GROUP_W), BF16),
        compiler_params=_params(1, 1),
    )(qi_arr, kj_arr, bias, qkv, qkv, qkv, _cum_matrix(tile))


def _conv_kernel(cb_ref, cc_ref, ch_ref, da_ref, dg_ref, par_ref, cw_ref, oc_ref, od_ref, u_ref, ud_ref,
                 ubuf, dbuf):
    t = pl.program_id(1)
    tile = cb_ref.shape[0]

    @pl.when(t == 0)
    def _():
        ubuf[0:SCONV_HIST, :] = jnp.zeros((SCONV_HIST, GROUP_W), F32)
        dbuf[0:CCONV_HIST, :] = jnp.zeros((CCONV_HIST, GROUP_W), F32)

    u = cc_ref[...] * ch_ref[...]
    ud = da_ref[...] * _sigmoid(dg_ref[...])
    u_ref[...] = u
    ud_ref[...] = ud
    ubuf[SCONV_HIST:SCONV_HIST + tile, :] = u
    dbuf[CCONV_HIST:CCONV_HIST + tile, :] = ud

    conv_c = par_ref[SCONV_W - 1:SCONV_W, :] * u
    for j in range(SCONV_W - 1):
        off = SCONV_HIST - (SCONV_W - 1) + j
        conv_c = conv_c + par_ref[j:j + 1, :] * ubuf[off:off + tile, :]
    oc_ref[...] = (cb_ref[...] * conv_c).astype(BF16)

    acc = cw_ref[CCONV_W - 1:CCONV_W, :] * ud + par_ref[3:4, :]
    for j in range(CCONV_W - 1):
        off = CCONV_HIST - (CCONV_W - 1) + j
        acc = acc + cw_ref[j:j + 1, :] * dbuf[off:off + tile, :]
    y = _layer_norm(acc, par_ref[4:5, :], par_ref[5:6, :])
    od_ref[...] = (y * _sigmoid(y)).astype(BF16)

    ubuf[0:SCONV_HIST, :] = ubuf[tile:tile + SCONV_HIST, :]
    dbuf[0:CCONV_HIST, :] = dbuf[tile:tile + CCONV_HIST, :]


def _conv_prompt(z, par, cw, n_batch, tp):
    tile = ATT_TILE
    nt = tp // tile

    def col(cb):
        return pl.BlockSpec((tile, GROUP_W), lambda b, t: (b * nt + t, cb))

    const = lambda b, t: (0, 0)
    out_spec = pl.BlockSpec((tile, GROUP_W), lambda b, t: (b * nt + t, 0))
    n_rows = n_batch * tp
    return pl.pallas_call(
        _conv_kernel,
        grid=(n_batch, nt),
        in_specs=[col(COL_CB), col(COL_CC), col(COL_CH), col(COL_DA), col(COL_DG),
                  pl.BlockSpec((8, GROUP_W), const), pl.BlockSpec((CCONV_HIST, GROUP_W), const)],
        out_specs=[out_spec, out_spec, out_spec, out_spec],
        out_shape=[jax.ShapeDtypeStruct((n_rows, GROUP_W), BF16), jax.ShapeDtypeStruct((n_rows, GROUP_W), BF16),
                   jax.ShapeDtypeStruct((n_rows, GROUP_W), F32), jax.ShapeDtypeStruct((n_rows, GROUP_W), F32)],
        scratch_shapes=[pltpu.VMEM((SCONV_HIST + tile, GROUP_W), F32),
                        pltpu.VMEM((CCONV_HIST + tile, GROUP_W), F32)],
        compiler_params=_params(1, 1),
    )(z, z, z, z, z, par, cw)


def _eye(n):
    return lax.broadcasted_iota(jnp.int32, (n, n), 0) == lax.broadcasted_iota(jnp.int32, (n, n), 1)


def _row_to_column(r):
    n = r.shape[1]
    return jnp.sum(jnp.where(_eye(n), jnp.broadcast_to(r, (n, n)), 0.0), axis=1, keepdims=True)


def _column_to_row(c):
    n = c.shape[0]
    return jnp.sum(jnp.where(_eye(n), jnp.broadcast_to(c, (n, n)), 0.0), axis=0, keepdims=True)


def _sample_attn_kernel(n_pages, pt_ref, bias_ref, z_ref, cum_ref, cross_ref, *refs):
    k_refs = refs[:n_pages]
    v_refs = refs[n_pages:2 * n_pages]
    o_ref = refs[2 * n_pages]
    page = k_refs[0].shape[2]
    rows = 8
    q = z_ref[:, COL_BQ * GROUP_W:(COL_BQ + 1) * GROUP_W] * (HEAD_DIM ** -0.5)
    q_cols = [jnp.broadcast_to(_row_to_column(q[:, h * HEAD_DIM:(h + 1) * HEAD_DIM]), (HEAD_DIM, page))
              for h in range(N_HEADS)]
    row = lax.broadcasted_iota(jnp.int32, (rows, 1), 0)
    tiles = []
    for p in range(n_pages):
        zp = jnp.zeros((rows, page), F32)
        for h in range(N_HEADS):
            zh = jnp.sum(q_cols[h] * k_refs[p][h], axis=0, keepdims=True) + bias_ref[h]
            zp = jnp.where(row == h, zh, zp)
        tiles.append(zp)
    z = jnp.concatenate(tiles, axis=0)
    sp = _softplus(z)
    hi, lo = _split2(sp)
    cum = _dot(hi, cum_ref[...]) + _dot(lo, cum_ref[...])
    carry = _dot_exact_rhs(cross_ref[...], jnp.broadcast_to(cum[:, 0:1], cum.shape))
    w = jnp.exp(z - cum - carry)
    outs = []
    for h in range(N_HEADS):
        acc = jnp.zeros((HEAD_DIM, page), F32)
        for p in range(n_pages):
            acc = acc + w[p * rows + h:p * rows + h + 1, :] * v_refs[p][h]
        outs.append(_column_to_row(jnp.sum(acc, axis=1, keepdims=True)))
    o_ref[...] = jnp.concatenate(outs, axis=1)


def _sample_attn(z3, cache_kt, cache_vt, page_table, bias, layer):
    n_seq = z3.shape[0]
    n_pages = page_table.shape[1]
    page = cache_kt.shape[4]
    n_cols = z3.shape[2]
    rows = 8
    idx = np.arange(n_pages * rows)
    cross = ((idx[:, None] % rows == idx[None, :] % rows) & (idx[None, :] // rows > idx[:, None] // rows))

    def page_spec(p):
        return pl.BlockSpec((None, None, N_HEADS, HEAD_DIM, page), lambda b, pt, bs: (pt[b, p], layer, 0, 0, 0))

    grid_spec = pltpu.PrefetchScalarGridSpec(
        num_scalar_prefetch=2,
        grid=(n_seq,),
        in_specs=[pl.BlockSpec((None, 1, n_cols), lambda b, pt, bs: (b, 0, 0)),
                  pl.BlockSpec((page, page), lambda b, pt, bs: (0, 0)),
                  pl.BlockSpec((n_pages * rows, n_pages * rows), lambda b, pt, bs: (0, 0))]
                 + [page_spec(p) for p in range(n_pages)] * 2,
        out_specs=pl.BlockSpec((None, 1, GROUP_W), lambda b, pt, bs: (b, 0, 0)),
    )
    out = pl.pallas_call(
        functools.partial(_sample_attn_kernel, n_pages),
        grid_spec=grid_spec,
        out_shape=jax.ShapeDtypeStruct((n_seq, 1, GROUP_W), F32),
        compiler_params=_params(1),
    )(page_table, bias, z3, _cum_matrix(page), jnp.asarray(cross.astype(np.float32), BF16),
      *([cache_kt] * n_pages), *([cache_vt] * n_pages))
    return out.reshape(n_seq, GROUP_W)


def _sample_hgrn_kernel(q_ref, f_ref, i_ref, g_ref, par_ref, s_ref, o_ref, sn_ref):
    logf, key = _hgrn_gates(f_ref[...], par_ref[:, 0:1], par_ref[:, 1:2], par_ref[:, 2:3])
    f = jnp.exp(logf)
    q = q_ref[...]
    v = i_ref[...]
    acc = jnp.zeros(v.shape, F32)
    for k in range(HEAD_DIM):
        s_new = f[k:k + 1, :] * s_ref[k] + key[k:k + 1, :] * v
        sn_ref[k] = s_new
        acc = acc + q[k:k + 1, :] * s_new
    ms = jnp.mean(acc * acc, axis=0, keepdims=True)
    g = g_ref[...]
    o_ref[...] = acc * lax.rsqrt(ms + RMS_EPS) * par_ref[:, 3:4] * (g * _sigmoid(g))


def _sample_hgrn(zt_s, state_t, par_t, layer):
    n_seq = zt_s.shape[1]

    def col(cb):
        return pl.BlockSpec((HEAD_DIM, n_seq), lambda h: (cb * N_HEADS + h, 0))

    return pl.pallas_call(
        _sample_hgrn_kernel,
        grid=(N_HEADS,),
        in_specs=[col(COL_AQ), col(COL_AF), col(COL_AI), col(COL_AG),
                  pl.BlockSpec((HEAD_DIM, 8), lambda h: (h, 0)),
                  pl.BlockSpec((None, None, HEAD_DIM, HEAD_DIM, n_seq), lambda h: (layer, h, 0, 0, 0))],
        out_specs=[pl.BlockSpec((HEAD_DIM, n_seq), lambda h: (h, 0)),
                   pl.BlockSpec((None, HEAD_DIM, HEAD_DIM, n_seq), lambda h: (h, 0, 0, 0))],
        out_shape=[jax.ShapeDtypeStruct((GROUP_W, n_seq), F32),
                   jax.ShapeDtypeStruct((N_HEADS, HEAD_DIM, HEAD_DIM, n_seq), F32)],
        compiler_params=_params(1),
    )(zt_s, zt_s, zt_s, zt_s, par_t, state_t)


def _sample_conv_kernel(z_ref, sc_ref, cc_ref, cpar_ref, cw_ref, oc_ref, od_ref, scn_ref, ccn_ref):
    nb = z_ref.shape[0]

    def zcol(cb):
        return z_ref[:, cb * GROUP_W:(cb + 1) * GROUP_W]

    u = zcol(COL_CC) * zcol(COL_CH)
    cb = zcol(COL_CB)
    ud = zcol(COL_DA) * _sigmoid(zcol(COL_DG))
    for i in range(nb):
        u_i = u[i:i + 1, :]
        conv_c = (cpar_ref[0:1, :] * sc_ref[i, 0:1, :] + cpar_ref[1:2, :] * sc_ref[i, 1:2, :]
                  + cpar_ref[2:3, :] * u_i)
        oc_ref[i:i + 1, :] = cb[i:i + 1, :] * conv_c
        scn_ref[i, 0:1, :] = sc_ref[i, 1:2, :]
        scn_ref[i, 1:2, :] = u_i

        ud_i = ud[i:i + 1, :]
        prev = cc_ref[i]
        conv_d = (jnp.sum(prev * cw_ref[0:CCONV_W - 1, :], axis=0, keepdims=True)
                  + cw_ref[CCONV_W - 1:CCONV_W, :] * ud_i + cpar_ref[3:4, :])
        y = _layer_norm(conv_d, cpar_ref[4:5, :], cpar_ref[5:6, :])
        od_ref[i:i + 1, :] = y * _sigmoid(y)
        ccn_ref[i, 0:CCONV_W - 2, :] = cc_ref[i, 1:CCONV_W - 1, :]
        ccn_ref[i, CCONV_W - 2:CCONV_W - 1, :] = ud_i


def _sample_conv(z_s, state_sconv, state_cconv, cpar, cw, layer):
    n_seq, n_cols = z_s.shape
    nb = SAMPLE_TILE
    const = lambda i: (0, 0)
    row_spec = pl.BlockSpec((nb, GROUP_W), lambda i: (i, 0))
    return pl.pallas_call(
        _sample_conv_kernel,
        grid=(n_seq // nb,),
        in_specs=[pl.BlockSpec((nb, n_cols), lambda i: (i, 0)),
                  pl.BlockSpec((nb, None, SCONV_W - 1, GROUP_W), lambda i: (i, layer, 0, 0)),
                  pl.BlockSpec((nb, None, CCONV_W - 1, GROUP_W), lambda i: (i, layer, 0, 0)),
                  pl.BlockSpec((8, GROUP_W), const),
                  pl.BlockSpec((CCONV_HIST, GROUP_W), const)],
        out_specs=[row_spec, row_spec,
                   pl.BlockSpec((nb, SCONV_W - 1, GROUP_W), lambda i: (i, 0, 0)),
                   pl.BlockSpec((nb, CCONV_W - 1, GROUP_W), lambda i: (i, 0, 0))],
        out_shape=[jax.ShapeDtypeStruct((n_seq, GROUP_W), F32)] * 2
                  + [jax.ShapeDtypeStruct((n_seq, SCONV_W - 1, GROUP_W), F32),
                     jax.ShapeDtypeStruct((n_seq, CCONV_W - 1, GROUP_W), F32)],
        compiler_params=_params(1),
    )(z_s, state_sconv, state_cconv, cpar, cw)


def _outproj_ln_kernel(alpha, oa_ref, ob_ref, oc_ref, od_ref, w_ref, x_ref, g_ref, b_ref, xo_ref, xb_ref):
    mix = _dot(oa_ref[...], w_ref[0:GROUP_W, :])
    for j, ref in enumerate((ob_ref, oc_ref, od_ref), start=1):
        mix = mix + _dot(ref[...], w_ref[j * GROUP_W:(j + 1) * GROUP_W, :])
    y = _layer_norm(alpha * x_ref[...] + mix, g_ref[...], b_ref[...])
    xo_ref[...] = y
    xb_ref[...] = y.astype(BF16)


def _outproj_ln(oa, ob, oc, od, w, x, g, b, alpha):
    n_rows, d = x.shape
    tm = _row_tile(n_rows, 1024, 128)
    part = pl.BlockSpec((tm, GROUP_W), lambda i: (i, 0))
    full = pl.BlockSpec((tm, d), lambda i: (i, 0))
    vec = pl.BlockSpec((1, d), lambda i: (0, 0))
    return pl.pallas_call(
        functools.partial(_outproj_ln_kernel, alpha),
        grid=(n_rows // tm,),
        in_specs=[part, part, part, part, pl.BlockSpec((4 * GROUP_W, d), lambda i: (0, 0)), full, vec, vec],
        out_specs=[full, full],
        out_shape=[jax.ShapeDtypeStruct((n_rows, d), F32), jax.ShapeDtypeStruct((n_rows, d), BF16)],
        compiler_params=_params(1),
    )(oa, ob, oc, od, w, x, g, b)


def _router_kernel(x_ref, w1_ref, w2_ref, w3_ref, b_ref, e_ref, g_ref):
    x1, x2, x3 = _split3(x_ref[...])
    w1, w2, w3 = w1_ref[...], w2_ref[...], w3_ref[...]
    logits = (_dot(x1, w1) + (_dot(x1, w2) + _dot(x2, w1)) + (_dot(x1, w3) + _dot(x2, w2) + _dot(x3, w1))
              + b_ref[...])
    n_lanes = logits.shape[1]
    lane = lax.broadcasted_iota(jnp.int32, (1, n_lanes), 1)
    e_out = jnp.zeros(logits.shape, jnp.int32)
    g_out = jnp.zeros(logits.shape, F32)
    denom = jnp.zeros((logits.shape[0], 1), F32)
    top = None
    for k in range(TOP_K):
        m = jnp.max(logits, axis=-1, keepdims=True)
        idx = jnp.min(jnp.where(logits == m, lane, n_lanes), axis=-1, keepdims=True)
        if top is None:
            top = m
        p = jnp.exp(m - top)
        denom = denom + p
        e_out = jnp.where(lane == k, idx, e_out)
        g_out = jnp.where(lane == k, p, g_out)
        logits = jnp.where(lane == idx, NEG_BIG * 2.0, logits)
    e_ref[...] = e_out
    g_ref[...] = g_out / denom


def _router(x, w_router, b_router):
    n_rows, d = x.shape
    n_exp = w_router.shape[1]
    lanes = 128
    wp = jnp.zeros((d, lanes), F32).at[:, :n_exp].set(w_router)
    bp = jnp.full((1, lanes), NEG_BIG, F32).at[0, :n_exp].set(b_router)
    w1, w2, w3 = _split3(wp)
    tm = _row_tile(n_rows, 1024, 128)
    wspec = pl.BlockSpec((d, lanes), lambda i: (0, 0))
    ospec = pl.BlockSpec((tm, lanes), lambda i: (i, 0))
    return pl.pallas_call(
        _router_kernel,
        grid=(n_rows // tm,),
        in_specs=[pl.BlockSpec((tm, d), lambda i: (i, 0)), wspec, wspec, wspec,
                  pl.BlockSpec((1, lanes), lambda i: (0, 0))],
        out_specs=[ospec, ospec],
        out_shape=[jax.ShapeDtypeStruct((n_rows, lanes), jnp.int32), jax.ShapeDtypeStruct((n_rows, lanes), F32)],
        compiler_params=_params(1),
    )(x, w1, w2, w3, bp)


def _expert_kernel(be_ref, nu_ref, idx_ref, x_hbm, wgu_ref, bgu_ref, wd_ref, bd_ref, y_hbm,
                   idx_smem, xbuf, ybuf, wgu_bf, wd_bf, sem_idx, sem_g, sem_s):
    i = pl.program_id(0)
    n = pl.num_programs(0)
    blk = xbuf.shape[1]
    d_exp = wd_ref.shape[0]

    def idx_copy(block, s):
        return pltpu.make_async_copy(idx_ref.at[block], idx_smem.at[s], sem_idx.at[s])

    def gather_copy(s, j, r):
        return pltpu.make_async_copy(x_hbm.at[pl.ds(r, 1)], xbuf.at[s, pl.ds(j, 1)], sem_g.at[s])

    def scatter_copy(s, j, r):
        return pltpu.make_async_copy(ybuf.at[s, pl.ds(j, 1)], y_hbm.at[pl.ds(r, 1)], sem_s.at[s])

    def start_gather(s):
        for j in range(blk):
            gather_copy(s, j, idx_smem[s, j]).start()

    def start_scatter(s):
        for j in range(blk):
            scatter_copy(s, j, idx_smem[s, blk + j]).start()

    def wait_rows(copy, s):
        for j in range(blk):
            copy(s, j, 0).wait()

    @pl.when(i == 0)
    def _():
        idx_copy(0, 0).start()
        idx_copy(0, 0).wait()
        idx_copy(1, 1).start()
        start_gather(0)

    def step(s):
        o = 1 - s
        wait_rows(gather_copy, s)
        idx_copy(i + 1, o).wait()

        @pl.when(i >= 2)
        def _():
            wait_rows(scatter_copy, s)

        changed = jnp.logical_or(i == 0, be_ref[i] != be_ref[jnp.maximum(i - 1, 0)])

        @pl.when(changed)
        def _():
            wgu_bf[...] = wgu_ref[...].astype(BF16)
            wd_bf[...] = wd_ref[...].astype(BF16)

        start_gather(o)

        @pl.when(i < nu_ref[0])
        def _():
            xb = xbuf[s].astype(BF16)
            gu = _dot(xb, wgu_bf[...]) + bgu_ref[...]
            gate = jnp.minimum(gu[:, :d_exp], SWIGLU_LIMIT)
            up = jnp.clip(gu[:, d_exp:], -SWIGLU_LIMIT, SWIGLU_LIMIT)
            act = (up + 1.0) * gate * _sigmoid(SWIGLU_ALPHA * gate)
            ybuf[s] = _dot(act.astype(BF16), wd_bf[...]) + bd_ref[...]

        @pl.when(i >= nu_ref[0])
        def _():
            ybuf[s] = jnp.zeros(ybuf.shape[1:], F32)

        start_scatter(s)
        idx_copy(i + 2, s).start()

        @pl.when(i == n - 1)
        def _():
            wait_rows(scatter_copy, s)
            wait_rows(gather_copy, o)
            idx_copy(i + 2, s).wait()

            @pl.when(n >= 2)
            def _():
                wait_rows(scatter_copy, o)

    pl.when(i % 2 == 0)(lambda: step(0))
    pl.when(i % 2 == 1)(lambda: step(1))


def _experts(x, idx, block_e, n_used, w_gate_up, b_gate_up, w_down, b_down, layer, n_slots):
    n_blocks = idx.shape[0] - 2
    blk = idx.shape[1] // 2
    d = x.shape[1]
    d2 = w_gate_up.shape[3]
    grid_spec = pltpu.PrefetchScalarGridSpec(
        num_scalar_prefetch=2,
        grid=(n_blocks,),
        in_specs=[pl.BlockSpec(idx.shape, lambda i, be, nu: (0, 0)),
                  pl.BlockSpec(memory_space=pl.ANY),
                  pl.BlockSpec((None, None, d, d2), lambda i, be, nu: (layer, be[i], 0, 0)),
                  pl.BlockSpec((None, None, 1, d2), lambda i, be, nu: (layer, be[i], 0, 0)),
                  pl.BlockSpec((None, None, d2 // 2, d), lambda i, be, nu: (layer, be[i], 0, 0)),
                  pl.BlockSpec((None, None, 1, d), lambda i, be, nu: (layer, be[i], 0, 0))],
        out_specs=pl.BlockSpec(memory_space=pl.ANY),
        scratch_shapes=[pltpu.SMEM((2, 2 * blk), jnp.int32),
                        pltpu.VMEM((2, blk, d), F32),
                        pltpu.VMEM((2, blk, d), F32),
                        pltpu.VMEM((d, d2), BF16),
                        pltpu.VMEM((d2 // 2, d), BF16),
                        pltpu.SemaphoreType.DMA((2,)),
                        pltpu.SemaphoreType.DMA((2,)),
                        pltpu.SemaphoreType.DMA((2,))],
    )
    return pl.pallas_call(
        _expert_kernel,
        grid_spec=grid_spec,
        out_shape=jax.ShapeDtypeStruct((n_slots, d), F32),
        compiler_params=_params(0, 1),
    )(block_e, n_used, idx, x, w_gate_up, b_gate_up, w_down, b_down)


def _route(top_e, n_rows, n_experts):
    blk = MOE_BLOCK
    n_assign = n_rows * TOP_K
    n_blocks = -(-n_assign // blk) + n_experts
    flat_e = top_e.reshape(-1)
    order = jnp.argsort(flat_e).astype(jnp.int32)
    sizes = jnp.sum(flat_e[:, None] == jnp.arange(n_experts, dtype=jnp.int32)[None, :], axis=0, dtype=jnp.int32)
    blocks_e = (sizes + blk - 1) // blk
    blk_end = jnp.cumsum(blocks_e)
    blk_start = blk_end - blocks_e
    grp_start = jnp.cumsum(sizes) - sizes
    block = jnp.arange(n_blocks, dtype=jnp.int32)
    block_e = jnp.minimum(jnp.sum(blk_end[None, :] <= block[:, None], axis=1, dtype=jnp.int32), n_experts - 1)
    lane = jnp.arange(blk, dtype=jnp.int32)[None, :]
    off = (block - blk_start[block_e])[:, None] * blk + lane
    valid = off < sizes[block_e][:, None]
    a = order[jnp.where(valid, grp_start[block_e][:, None] + off, 0)]
    tok = a // TOP_K
    src = jnp.where(valid, tok, 0)
    dump = n_assign + (block % 2)[:, None] * blk + lane
    dst = jnp.where(valid, (a % TOP_K) * n_rows + tok, dump)
    idx = jnp.concatenate([src, dst], axis=1)
    idx = jnp.concatenate([idx, jnp.zeros((2, 2 * blk), jnp.int32)], axis=0)
    n_used = blk_end[-1:].astype(jnp.int32)
    return idx, block_e, n_used, n_assign + 2 * blk


def _combine_ln_kernel(alpha, x_ref, y0_ref, y1_ref, y2_ref, y3_ref, gate_ref, g_ref, b_ref, xo_ref, xb_ref):
    gates = gate_ref[...]
    mix = gates[:, 0:1] * y0_ref[...]
    for k, ref in enumerate((y1_ref, y2_ref, y3_ref), start=1):
        mix = mix + gates[:, k:k + 1] * ref[...]
    y = _layer_norm(alpha * x_ref[...] + mix, g_ref[...], b_ref[...])
    xo_ref[...] = y
    xb_ref[...] = y.astype(BF16)


def _combine_ln(x, y_slots, gates, g, b, alpha):
    n_rows, d = x.shape
    tm = _row_tile(n_rows, 512, 64)
    nt = n_rows // tm
    full = pl.BlockSpec((tm, d), lambda i: (i, 0))
    vec = pl.BlockSpec((1, d), lambda i: (0, 0))

    def yspec(k):
        return pl.BlockSpec((tm, d), lambda i: (k * nt + i, 0))

    return pl.pallas_call(
        functools.partial(_combine_ln_kernel, alpha),
        grid=(nt,),
        in_specs=[full, yspec(0), yspec(1), yspec(2), yspec(3),
                  pl.BlockSpec((tm, gates.shape[1]), lambda i: (i, 0)), vec, vec],
        out_specs=[full, full],
        out_shape=[jax.ShapeDtypeStruct((n_rows, d), F32), jax.ShapeDtypeStruct((n_rows, d), BF16)],
        compiler_params=_params(1),
    )(x, y_slots, y_slots, y_slots, y_slots, gates, g, b)


def _pad_rows(a, n):
    return jnp.zeros((n, a.shape[1]), a.dtype).at[:a.shape[0]].set(a)


def kernel(x_prompt, x_sample, cache_sb_k, cache_sb_v, page_table, state_hgrn, state_sconv, state_cconv, meta_tokens, w_in, w_out, sb_bias, hg_lb_logits, hg_norm_w, sconv_w, cconv_w, cconv_b, cconv_ln_g, cconv_ln_b, ln1_g, ln1_b, w_router, b_router, w_gate_up, b_gate_up, w_down, b_down, ln2_g, ln2_b):
    n_batch, seq, d = x_prompt.shape
    n_seq = x_sample.shape[0]
    depth = w_in.shape[0]
    n_meta = meta_tokens.shape[0]
    n_experts = w_router.shape[2]
    t_len = n_meta + seq
    tp = -(-t_len // ATT_TILE) * ATT_TILE
    n_prompt_rows = n_batch * tp
    n_rows = n_prompt_rows + n_seq
    alpha = float((2 * depth) ** 0.25)
    assert d == 4 * GROUP_W and x_sample.shape[1] == 1 and n_seq % SAMPLE_TILE == 0

    pieces = []
    for b in range(n_batch):
        pieces += [meta_tokens.astype(F32), x_prompt[b], jnp.zeros((tp - t_len, d), F32)]
    pieces.append(x_sample.reshape(n_seq, d))
    x = jnp.concatenate(pieces, axis=0)
    xb = x.astype(BF16)

    lb_cum = jnp.cumsum(jax.nn.softmax(hg_lb_logits.astype(F32), axis=0), axis=0)
    lb_all = lb_cum - lb_cum[0]

    cache_kt = jnp.transpose(cache_sb_k, (0, 1, 3, 4, 2))
    cache_vt = jnp.transpose(cache_sb_v, (0, 1, 3, 4, 2))
    state_t = jnp.transpose(state_hgrn, (1, 2, 3, 4, 0))

    hg_p, hg_s, kp_l, vp_l, ks_l, vs_l, sc_p, sc_s, cc_p, cc_s = ([] for _ in range(10))
    for l in range(depth):
        lb = lb_all[l]
        hpar = _pad_rows(jnp.stack([jnp.log(lb), jnp.log1p(-lb), 1.0 - lb, hg_norm_w[l]]), 8)
        cpar = _pad_rows(jnp.concatenate([sconv_w[l], cconv_b[l][None], cconv_ln_g[l][None],
                                          cconv_ln_b[l][None]], axis=0), 8)
        cw = _pad_rows(cconv_w[l], CCONV_HIST)

        z, qkv = _inproj(xb, w_in[l].astype(BF16))

        oa_p, st = _hgrn_prompt(z, hpar, n_batch, tp, t_len)
        ob_p = _sbattn_prompt(qkv, sb_bias[l], n_batch, tp)
        oc_p, od_p, u_p, ud_p = _conv_prompt(z, cpar, cw, n_batch, tp)

        z_s = z[n_prompt_rows:]
        ob_s = _sample_attn(z_s.reshape(n_seq, 1, z.shape[1]), cache_kt, cache_vt, page_table, sb_bias[l], l)
        oa_t, s_new = _sample_hgrn(z_s.T, state_t, hpar.T, l)
        oa_s = oa_t.T
        oc_s, od_s, sc_new, cc_new = _sample_conv(z_s, state_sconv, state_cconv, cpar, cw, l)

        oa = jnp.concatenate([oa_p, oa_s.astype(BF16)], axis=0)
        ob = jnp.concatenate([ob_p, ob_s.astype(BF16)], axis=0)
        oc = jnp.concatenate([oc_p, oc_s.astype(BF16)], axis=0)
        od = jnp.concatenate([od_p, od_s.astype(BF16)], axis=0)
        x, xb = _outproj_ln(oa, ob, oc, od, w_out[l].astype(BF16), x, ln1_g[l][None], ln1_b[l][None], alpha)

        top_e, gates = _router(x, w_router[l], b_router[l])
        idx, block_e, n_used, n_slots = _route(top_e[:, :TOP_K], n_rows, n_experts)
        y_slots = _experts(x, idx, block_e, n_used, w_gate_up, b_gate_up.reshape(depth, n_experts, 1, -1),
                           w_down, b_down.reshape(depth, n_experts, 1, -1), l, n_slots)
        x, xb = _combine_ln(x, y_slots, gates, ln2_g[l][None], ln2_b[l][None], alpha)

        zp = z[:n_prompt_rows].reshape(n_batch, tp, -1)
        st4 = st.reshape(n_batch, N_HEADS, HEAD_DIM, N_HEADS, HEAD_DIM)
        hg_p.append(jnp.stack([st4[:, h, :, h, :] for h in range(N_HEADS)], axis=1).swapaxes(-1, -2))
        hg_s.append(s_new)
        kp_l.append(zp[:, :t_len, COL_BK * GROUP_W:(COL_BK + 1) * GROUP_W].reshape(n_batch, t_len, N_HEADS, HEAD_DIM))
        vp_l.append(zp[:, :t_len, COL_BV * GROUP_W:(COL_BV + 1) * GROUP_W].reshape(n_batch, t_len, N_HEADS, HEAD_DIM))
        ks_l.append(z_s[:, COL_BK * GROUP_W:(COL_BK + 1) * GROUP_W].reshape(n_seq, 1, N_HEADS, HEAD_DIM))
        vs_l.append(z_s[:, COL_BV * GROUP_W:(COL_BV + 1) * GROUP_W].reshape(n_seq, 1, N_HEADS, HEAD_DIM))
        sc_p.append(u_p.reshape(n_batch, tp, GROUP_W)[:, t_len - (SCONV_W - 1):t_len])
        cc_p.append(ud_p.reshape(n_batch, tp, GROUP_W)[:, t_len - (CCONV_W - 1):t_len])
        sc_s.append(sc_new)
        cc_s.append(cc_new)

    y_prompt = x[:n_prompt_rows].reshape(n_batch, tp, d)[:, n_meta:t_len]
    y_sample = x[n_prompt_rows:].reshape(n_seq, 1, d)
    return (y_prompt, y_sample,
            jnp.stack(hg_p, axis=1), jnp.transpose(jnp.stack(hg_s, axis=0), (4, 0, 1, 2, 3)),
            jnp.stack(kp_l, axis=1), jnp.stack(vp_l, axis=1),
            jnp.stack(ks_l, axis=1), jnp.stack(vs_l, axis=1),
            jnp.stack(sc_p, axis=1), jnp.stack(sc_s, axis=1),
            jnp.stack(cc_p, axis=1), jnp.stack(cc_s, axis=1))
```

```python
import functools

import numpy as np
import jax
import jax.numpy as jnp
from jax import lax
from jax.experimental import pallas as pl
from jax.experimental.pallas import tpu as pltpu

F32 = jnp.float32
BF16 = jnp.bfloat16

HEAD_DIM = 64
N_HEADS = 4
GROUP_W = HEAD_DIM * N_HEADS
HG_CHUNK = 64
HG_MID = HG_CHUNK // 2 - 1
ATT_TILE = 256
ATT_ROWS = 256
TOP_K = 4
MOE_BLOCK = 256
SWIGLU_LIMIT = 7.0
SWIGLU_ALPHA = 1.702
LN_EPS = 1e-5
RMS_EPS = 1e-6
CCONV_W = 31
SCONV_W = 3
CCONV_HIST = 32
SCONV_HIST = 8
SAMPLE_TILE = 16
VMEM_LIMIT = 56 * 1024 * 1024
NEG_BIG = -1e30
LOG2E = 1.4426950408889634
ROW_CHUNKS = 8

COL_AQ, COL_AF, COL_AI, COL_AG, COL_BQ, COL_BK, COL_BV, COL_CB, COL_CC, COL_CH, COL_DA, COL_DG = range(12)


def _dot(a, b):
    return jnp.dot(a, b, preferred_element_type=F32)


def _dot_nt(a, b):
    return lax.dot_general(a, b, (((1,), (1,)), ((), ())), preferred_element_type=F32)


def _dot_tn(a, b):
    return lax.dot_general(a, b, (((0,), (0,)), ((), ())), preferred_element_type=F32)


def _split2(x):
    hi = x.astype(BF16)
    lo = (x - hi.astype(x.dtype)).astype(BF16)
    return hi, lo


def _split3(x):
    h1 = x.astype(BF16)
    r1 = x - h1.astype(x.dtype)
    h2 = r1.astype(BF16)
    h3 = (r1 - h2.astype(x.dtype)).astype(BF16)
    return h1, h2, h3


def _dot_exact_rhs(a_bf16, x):
    h1, h2, h3 = _split3(x)
    return _dot(a_bf16, h1) + _dot(a_bf16, h2) + _dot(a_bf16, h3)


def _dot_exact_lhs(x, a_bf16):
    h1, h2, h3 = _split3(x)
    return _dot(h1, a_bf16) + _dot(h2, a_bf16) + _dot(h3, a_bf16)


def _sigmoid(x):
    return 1.0 / (1.0 + jnp.exp(-x))


def _softplus(z):
    return jnp.maximum(z, 0.0) + jnp.log(1.0 + jnp.exp(-jnp.abs(z)))


def _layer_norm(y, g, b):
    mu = jnp.mean(y, axis=-1, keepdims=True)
    d = y - mu
    var = jnp.mean(d * d, axis=-1, keepdims=True)
    return d * lax.rsqrt(var + LN_EPS) * g + b


def _row_tile(n_rows, cap, mult):
    best = None
    for t in range(mult, cap + 1, mult):
        if n_rows % t == 0:
            best = t
    assert best is not None, (n_rows, cap, mult)
    return best


def _params(n_parallel, n_arbitrary=0):
    sem = ("parallel",) * n_parallel + ("arbitrary",) * n_arbitrary
    return pltpu.CompilerParams(dimension_semantics=sem, vmem_limit_bytes=VMEM_LIMIT)


def _inproj_kernel(x_ref, w_ref, z_ref, qkv_ref):
    z = _dot(x_ref[...], w_ref[...])
    z_ref[...] = z
    qkv_ref[...] = z[:, COL_BQ * GROUP_W:(COL_BV + 1) * GROUP_W].astype(BF16)


def _inproj(xb, w):
    n_rows, d = xb.shape
    n_cols = w.shape[1]
    tm = _row_tile(n_rows, 512, 64)
    return pl.pallas_call(
        _inproj_kernel,
        grid=(n_rows // tm,),
        in_specs=[pl.BlockSpec((tm, d), lambda i: (i, 0)),
                  pl.BlockSpec((d, n_cols), lambda i: (0, 0))],
        out_specs=[pl.BlockSpec((tm, n_cols), lambda i: (i, 0)),
                   pl.BlockSpec((tm, 3 * GROUP_W), lambda i: (i, 0))],
        out_shape=[jax.ShapeDtypeStruct((n_rows, n_cols), F32),
                   jax.ShapeDtypeStruct((n_rows, 3 * GROUP_W), BF16)],
        compiler_params=_params(1),
    )(xb, w)


def _hgrn_gates(fl, log_lb, log_1m_lb, one_m_lb):
    e = jnp.exp(-jnp.abs(fl))
    log_sig = jnp.minimum(fl, 0.0) - jnp.log1p(e)
    a = log_lb
    b = log_1m_lb + log_sig
    logf = jnp.maximum(a, b) + jnp.log1p(jnp.exp(-jnp.abs(a - b)))
    key = one_m_lb * (jnp.where(fl >= 0.0, e, 1.0) / (1.0 + e))
    return logf, key


def _hgrn_kernel(n_valid, q_ref, f_ref, i_ref, g_ref, par_ref, tri_ref, hones_ref, o_ref, st_ref, st_scr, o_scr):
    t = pl.program_id(1)
    tile = q_ref.shape[0]

    @pl.when(t == 0)
    def _():
        st_scr[...] = jnp.zeros_like(st_scr)

    logf, key = _hgrn_gates(f_ref[...], par_ref[0:1, :], par_ref[1:2, :], par_ref[2:3, :])
    row = t * tile + lax.broadcasted_iota(jnp.int32, (tile, 1), 0)
    valid = row < n_valid
    logf = jnp.where(valid, logf, 0.0)
    key = jnp.where(valid, key, 0.0)
    c = _dot_exact_rhs(tri_ref[...], logf)
    q = q_ref[...]
    v = i_ref[...]

    lane_head = lax.broadcasted_iota(jnp.int32, (1, GROUP_W), 1) // HEAD_DIM
    r_head = lax.broadcasted_iota(jnp.int32, (GROUP_W, 1), 0) // HEAD_DIM
    same_head = r_head == lane_head
    t_in = lax.broadcasted_iota(jnp.int32, (N_HEADS * HG_CHUNK, 1), 0) % HG_CHUNK
    s_in = lax.broadcasted_iota(jnp.int32, (1, HG_CHUNK), 1)
    causal = s_in <= t_in

    for j in range(tile // HG_CHUNK):
        r0 = j * HG_CHUNK
        cj = c[r0:r0 + HG_CHUNK]
        c_mid = cj[HG_MID:HG_MID + 1]
        c_last = cj[HG_CHUNK - 1:HG_CHUNK]
        qj = q[r0:r0 + HG_CHUNK]
        kj = key[r0:r0 + HG_CHUNK]
        vj = v[r0:r0 + HG_CHUNK].astype(BF16)
        q_mid = qj * jnp.exp(cj - c_mid)
        k_mid = (kj * jnp.exp(c_mid - cj)).astype(BF16)
        q_dec = (qj * jnp.exp(cj)).astype(BF16)
        k_dec = (kj * jnp.exp(c_last - cj)).astype(BF16)
        decay = jnp.exp(c_last)
        q_heads = jnp.concatenate([jnp.where(lane_head == h, q_mid, 0.0) for h in range(N_HEADS)], axis=0)
        scores = _dot_nt(q_heads.astype(BF16), k_mid)
        scores = jnp.where(causal, scores, 0.0).astype(BF16)
        o_heads = _dot(scores, vj)
        o_intra = jnp.where(lane_head == 0, o_heads[0:HG_CHUNK], 0.0)
        for h in range(1, N_HEADS):
            o_intra = o_intra + jnp.where(lane_head == h, o_heads[h * HG_CHUNK:(h + 1) * HG_CHUNK], 0.0)
        st = st_scr[...]
        o_inter = _dot_nt(q_dec, st.astype(BF16))
        o_scr[r0:r0 + HG_CHUNK, :] = o_intra + o_inter
        st_scr[...] = st * decay + jnp.where(same_head, _dot_tn(vj, k_dec), 0.0)

    o = o_scr[...]
    sq_hi, sq_lo = _split2(o * o)
    ms = (_dot(sq_hi, hones_ref[...]) + _dot(sq_lo, hones_ref[...])) * (1.0 / HEAD_DIM)
    on = o * lax.rsqrt(ms + RMS_EPS) * par_ref[3:4, :]
    g = g_ref[...]
    o_ref[...] = (on * (g * _sigmoid(g))).astype(BF16)

    @pl.when(t == pl.num_programs(1) - 1)
    def _():
        st_ref[0] = st_scr[...]


def _hgrn_prompt(z, par, n_batch, tp, n_valid):
    tile = ATT_TILE
    nt = tp // tile
    tri = np.zeros((tile, tile), np.float32)
    idx = np.arange(tile)
    tri[(idx[:, None] // HG_CHUNK == idx[None, :] // HG_CHUNK) & (idx[None, :] <= idx[:, None])] = 1.0
    hones = (idx[:, None] // HEAD_DIM == idx[None, :] // HEAD_DIM).astype(np.float32)

    def col(cb):
        return pl.BlockSpec((tile, GROUP_W), lambda b, t: (b * nt + t, cb))

    const = lambda b, t: (0, 0)
    return pl.pallas_call(
        functools.partial(_hgrn_kernel, n_valid),
        grid=(n_batch, nt),
        in_specs=[col(COL_AQ), col(COL_AF), col(COL_AI), col(COL_AG),
                  pl.BlockSpec((8, GROUP_W), const),
                  pl.BlockSpec((tile, tile), const),
                  pl.BlockSpec((GROUP_W, GROUP_W), const)],
        out_specs=[pl.BlockSpec((tile, GROUP_W), lambda b, t: (b * nt + t, 0)),
                   pl.BlockSpec((1, GROUP_W, GROUP_W), lambda b, t: (b, 0, 0))],
        out_shape=[jax.ShapeDtypeStruct((n_batch * tp, GROUP_W), BF16),
                   jax.ShapeDtypeStruct((n_batch, GROUP_W, GROUP_W), F32)],
        scratch_shapes=[pltpu.VMEM((GROUP_W, GROUP_W), F32), pltpu.VMEM((tile, GROUP_W), F32)],
        compiler_params=_params(1, 1),
    )(z, z, z, z, par, jnp.asarray(tri, BF16), jnp.asarray(hones, BF16))


def _sb_tile(z2, valid, carry, cum_ref):
    neg_abs = lax.bitcast_convert_type(lax.bitcast_convert_type(z2, jnp.uint32) | jnp.uint32(0x80000000), F32)
    sp = jnp.maximum(z2, 0.0) + jnp.log2(1.0 + jnp.exp2(neg_abs))
    if valid is not None:
        sp = jnp.where(valid, sp, 0.0)
    hi, lo = _split2(sp)
    cum = _dot(hi, cum_ref[...]) + _dot(lo, cum_ref[...])
    w = jnp.exp2(z2 - cum - carry)
    if valid is not None:
        w = jnp.where(valid, w, 0.0)
    return w, carry + cum[:, 0:1]


def _sbattn_kernel(ta_ref, tb_ref, kj_ref, pair_ref, bias_ref, qa_ref, qb_ref, k_ref, v_ref, cum_ref,
                   oa_ref, ob_ref, qm_scr, acc_scr, carry_scr):
    del pair_ref
    s = pl.program_id(1)
    ta = ta_ref[s]
    tb = tb_ref[s]
    kj = kj_ref[s]
    tile = qa_ref.shape[0]
    lane_head = lax.broadcasted_iota(jnp.int32, (1, GROUP_W), 1) // HEAD_DIM

    @pl.when(kj == tb)
    def _():
        acc_scr[...] = jnp.zeros_like(acc_scr)
        carry_scr[...] = jnp.zeros_like(carry_scr)
        q = jnp.concatenate([qa_ref[...], qb_ref[...]], axis=0)
        for h in range(N_HEADS):
            qm_scr[h] = jnp.where(lane_head == h, q * (LOG2E * HEAD_DIM ** -0.5), 0.0).astype(BF16)

    def body(masked):
        k = k_ref[...]
        v = v_ref[...]
        valid = None
        if masked:
            r = lax.broadcasted_iota(jnp.int32, (2 * tile, 1), 0)
            q_pos = jnp.where(r < tile, ta * tile + r, tb * tile + r - tile)
            valid = kj * tile + lax.broadcasted_iota(jnp.int32, (1, tile), 1) < q_pos
        pv = None
        for h in range(N_HEADS):
            z2 = _dot_nt(qm_scr[h], k) + bias_ref[h] * LOG2E
            w, carry = _sb_tile(z2, valid, carry_scr[h], cum_ref)
            carry_scr[h] = carry
            pv_h = _dot(w.astype(BF16), jnp.where(lane_head == h, v, jnp.zeros_like(v)))
            pv = pv_h if pv is None else pv + pv_h
        acc_scr[...] += pv

    pl.when(kj >= ta)(lambda: body(True))
    pl.when(kj < ta)(lambda: body(False))

    @pl.when(kj == 0)
    def _():
        oa_ref[...] = acc_scr[0:tile, :].astype(BF16)
        ob_ref[...] = acc_scr[tile:2 * tile, :].astype(BF16)


def _cum_matrix(n):
    idx = np.arange(n)
    return jnp.asarray((idx[:, None] >= idx[None, :]).astype(np.float32), BF16)


def _sbattn_prompt(qkv, bias, n_batch, tp):
    tile = ATT_TILE
    nq = tp // tile
    n_pairs = -(-nq // 2)
    ta_list, tb_list, kj_list, pair_list = [], [], [], []
    for p in range(n_pairs):
        ta, tb = 2 * p, min(2 * p + 1, nq - 1)
        for kj in range(tb, -1, -1):
            ta_list.append(ta)
            tb_list.append(tb)
            kj_list.append(kj)
            pair_list.append(p)
    as_arr = lambda v: jnp.asarray(np.array(v, np.int32))
    out_spec = pl.BlockSpec((tile, GROUP_W), lambda b, s, ta, tb, kj, pr, bs: (b * n_pairs + pr[s], 0))
    grid_spec = pltpu.PrefetchScalarGridSpec(
        num_scalar_prefetch=5,
        grid=(n_batch, len(kj_list)),
        in_specs=[pl.BlockSpec((tile, GROUP_W), lambda b, s, ta, tb, kj, pr, bs: (b * nq + ta[s], 0)),
                  pl.BlockSpec((tile, GROUP_W), lambda b, s, ta, tb, kj, pr, bs: (b * nq + tb[s], 0)),
                  pl.BlockSpec((tile, GROUP_W), lambda b, s, ta, tb, kj, pr, bs: (b * nq + kj[s], 1)),
                  pl.BlockSpec((tile, GROUP_W), lambda b, s, ta, tb, kj, pr, bs: (b * nq + kj[s], 2)),
                  pl.BlockSpec((tile, tile), lambda b, s, ta, tb, kj, pr, bs: (0, 0))],
        out_specs=[out_spec, out_spec],
        scratch_shapes=[pltpu.VMEM((N_HEADS, 2 * tile, GROUP_W), BF16),
                        pltpu.VMEM((2 * tile, GROUP_W), F32),
                        pltpu.VMEM((N_HEADS, 2 * tile, 1), F32)],
    )
    o_a, o_b = pl.pallas_call(
        _sbattn_kernel,
        grid_spec=grid_spec,
        out_shape=[jax.ShapeDtypeStruct((n_batch * n_pairs * tile, GROUP_W), BF16)] * 2,
        compiler_params=_params(1, 1),
    )(as_arr(ta_list), as_arr(tb_list), as_arr(kj_list), as_arr(pair_list), bias, qkv, qkv, qkv, qkv,
      _cum_matrix(tile))
    o = jnp.stack([o_a.reshape(n_batch, n_pairs, tile, GROUP_W), o_b.reshape(n_batch, n_pairs, tile, GROUP_W)],
                  axis=2)
    return o.reshape(n_batch, 2 * n_pairs * tile, GROUP_W)[:, :tp].reshape(n_batch * tp, GROUP_W)


def _conv_kernel(cb_ref, cc_ref, ch_ref, da_ref, dg_ref, par_ref, cw_ref, oc_ref, od_ref, u_ref, ud_ref,
                 ubuf, dbuf):
    t = pl.program_id(1)
    tile = cb_ref.shape[0]

    @pl.when(t == 0)
    def _():
        ubuf[0:SCONV_HIST, :] = jnp.zeros((SCONV_HIST, GROUP_W), F32)
        dbuf[0:CCONV_HIST, :] = jnp.zeros((CCONV_HIST, GROUP_W), F32)

    u = cc_ref[...] * ch_ref[...]
    ud = da_ref[...] * _sigmoid(dg_ref[...])
    u_ref[...] = u
    ud_ref[...] = ud
    ubuf[SCONV_HIST:SCONV_HIST + tile, :] = u
    dbuf[CCONV_HIST:CCONV_HIST + tile, :] = ud

    conv_c = par_ref[SCONV_W - 1:SCONV_W, :] * u
    for j in range(SCONV_W - 1):
        off = SCONV_HIST - (SCONV_W - 1) + j
        conv_c = conv_c + par_ref[j:j + 1, :] * ubuf[off:off + tile, :]
    oc_ref[...] = (cb_ref[...] * conv_c).astype(BF16)

    acc = cw_ref[CCONV_W - 1:CCONV_W, :] * ud + par_ref[3:4, :]
    for j in range(CCONV_W - 1):
        off = CCONV_HIST - (CCONV_W - 1) + j
        acc = acc + cw_ref[j:j + 1, :] * dbuf[off:off + tile, :]
    y = _layer_norm(acc, par_ref[4:5, :], par_ref[5:6, :])
    od_ref[...] = (y * _sigmoid(y)).astype(BF16)

    ubuf[0:SCONV_HIST, :] = ubuf[tile:tile + SCONV_HIST, :]
    dbuf[0:CCONV_HIST, :] = dbuf[tile:tile + CCONV_HIST, :]


def _conv_prompt(z, par, cw, n_batch, tp):
    tile = ATT_TILE
    nt = tp // tile

    def col(cb):
        return pl.BlockSpec((tile, GROUP_W), lambda b, t: (b * nt + t, cb))

    const = lambda b, t: (0, 0)
    out_spec = pl.BlockSpec((tile, GROUP_W), lambda b, t: (b * nt + t, 0))
    n_rows = n_batch * tp
    return pl.pallas_call(
        _conv_kernel,
        grid=(n_batch, nt),
        in_specs=[col(COL_CB), col(COL_CC), col(COL_CH), col(COL_DA), col(COL_DG),
                  pl.BlockSpec((8, GROUP_W), const), pl.BlockSpec((CCONV_HIST, GROUP_W), const)],
        out_specs=[out_spec, out_spec, out_spec, out_spec],
        out_shape=[jax.ShapeDtypeStruct((n_rows, GROUP_W), BF16), jax.ShapeDtypeStruct((n_rows, GROUP_W), BF16),
                   jax.ShapeDtypeStruct((n_rows, GROUP_W), F32), jax.ShapeDtypeStruct((n_rows, GROUP_W), F32)],
        scratch_shapes=[pltpu.VMEM((SCONV_HIST + tile, GROUP_W), F32),
                        pltpu.VMEM((CCONV_HIST + tile, GROUP_W), F32)],
        compiler_params=_params(1, 1),
    )(z, z, z, z, z, par, cw)


def _eye(n):
    return lax.broadcasted_iota(jnp.int32, (n, n), 0) == lax.broadcasted_iota(jnp.int32, (n, n), 1)


def _row_to_column(r):
    n = r.shape[1]
    return jnp.sum(jnp.where(_eye(n), jnp.broadcast_to(r, (n, n)), 0.0), axis=1, keepdims=True)


def _column_to_row(c):
    n = c.shape[0]
    return jnp.sum(jnp.where(_eye(n), jnp.broadcast_to(c, (n, n)), 0.0), axis=0, keepdims=True)


def _sample_attn_kernel(n_pages, pt_ref, bias_ref, z_ref, cum_ref, cross_ref, *refs):
    k_refs = refs[:n_pages]
    v_refs = refs[n_pages:2 * n_pages]
    o_ref = refs[2 * n_pages]
    page = k_refs[0].shape[2]
    rows = 8
    q = z_ref[:, COL_BQ * GROUP_W:(COL_BQ + 1) * GROUP_W] * (HEAD_DIM ** -0.5)
    q_cols = [jnp.broadcast_to(_row_to_column(q[:, h * HEAD_DIM:(h + 1) * HEAD_DIM]), (HEAD_DIM, page))
              for h in range(N_HEADS)]
    row = lax.broadcasted_iota(jnp.int32, (rows, 1), 0)
    tiles = []
    for p in range(n_pages):
        zp = jnp.zeros((rows, page), F32)
        for h in range(N_HEADS):
            zh = jnp.sum(q_cols[h] * k_refs[p][h], axis=0, keepdims=True) + bias_ref[h]
            zp = jnp.where(row == h, zh, zp)
        tiles.append(zp)
    z = jnp.concatenate(tiles, axis=0)
    sp = _softplus(z)
    hi, lo = _split2(sp)
    cum = _dot(hi, cum_ref[...]) + _dot(lo, cum_ref[...])
    carry = _dot_exact_rhs(cross_ref[...], jnp.broadcast_to(cum[:, 0:1], cum.shape))
    w = jnp.exp(z - cum - carry)
    outs = []
    for h in range(N_HEADS):
        acc = jnp.zeros((HEAD_DIM, page), F32)
        for p in range(n_pages):
            acc = acc + w[p * rows + h:p * rows + h + 1, :] * v_refs[p][h]
        outs.append(_column_to_row(jnp.sum(acc, axis=1, keepdims=True)))
    o_ref[...] = jnp.concatenate(outs, axis=1)


def _sample_attn(z3, cache_kt, cache_vt, page_table, bias, layer):
    n_seq = z3.shape[0]
    n_pages = page_table.shape[1]
    page = cache_kt.shape[4]
    n_cols = z3.shape[2]
    rows = 8
    idx = np.arange(n_pages * rows)
    cross = ((idx[:, None] % rows == idx[None, :] % rows) & (idx[None, :] // rows > idx[:, None] // rows))

    def page_spec(p):
        return pl.BlockSpec((None, None, N_HEADS, HEAD_DIM, page), lambda b, pt, bs: (pt[b, p], layer, 0, 0, 0))

    grid_spec = pltpu.PrefetchScalarGridSpec(
        num_scalar_prefetch=2,
        grid=(n_seq,),
        in_specs=[pl.BlockSpec((None, 1, n_cols), lambda b, pt, bs: (b, 0, 0)),
                  pl.BlockSpec((page, page), lambda b, pt, bs: (0, 0)),
                  pl.BlockSpec((n_pages * rows, n_pages * rows), lambda b, pt, bs: (0, 0))]
                 + [page_spec(p) for p in range(n_pages)] * 2,
        out_specs=pl.BlockSpec((None, 1, GROUP_W), lambda b, pt, bs: (b, 0, 0)),
    )
    out = pl.pallas_call(
        functools.partial(_sample_attn_kernel, n_pages),
        grid_spec=grid_spec,
        out_shape=jax.ShapeDtypeStruct((n_seq, 1, GROUP_W), F32),
        compiler_params=_params(1),
    )(page_table, bias, z3, _cum_matrix(page), jnp.asarray(cross.astype(np.float32), BF16),
      *([cache_kt] * n_pages), *([cache_vt] * n_pages))
    return out.reshape(n_seq, GROUP_W)


def _sample_hgrn_kernel(q_ref, f_ref, i_ref, g_ref, par_ref, s_ref, o_ref, sn_ref):
    logf, key = _hgrn_gates(f_ref[...], par_ref[:, 0:1], par_ref[:, 1:2], par_ref[:, 2:3])
    f = jnp.exp(logf)
    q = q_ref[...]
    v = i_ref[...]
    acc = jnp.zeros(v.shape, F32)
    for k in range(HEAD_DIM):
        s_new = f[k:k + 1, :] * s_ref[k] + key[k:k + 1, :] * v
        sn_ref[k] = s_new
        acc = acc + q[k:k + 1, :] * s_new
    ms = jnp.mean(acc * acc, axis=0, keepdims=True)
    g = g_ref[...]
    o_ref[...] = acc * lax.rsqrt(ms + RMS_EPS) * par_ref[:, 3:4] * (g * _sigmoid(g))


def _sample_hgrn(zt_s, state_t, par_t, layer):
    n_seq = zt_s.shape[1]

    def col(cb):
        return pl.BlockSpec((HEAD_DIM, n_seq), lambda h: (cb * N_HEADS + h, 0))

    return pl.pallas_call(
        _sample_hgrn_kernel,
        grid=(N_HEADS,),
        in_specs=[col(COL_AQ), col(COL_AF), col(COL_AI), col(COL_AG),
                  pl.BlockSpec((HEAD_DIM, 8), lambda h: (h, 0)),
                  pl.BlockSpec((None, None, HEAD_DIM, HEAD_DIM, n_seq), lambda h: (layer, h, 0, 0, 0))],
        out_specs=[pl.BlockSpec((HEAD_DIM, n_seq), lambda h: (h, 0)),
                   pl.BlockSpec((None, HEAD_DIM, HEAD_DIM, n_seq), lambda h: (h, 0, 0, 0))],
        out_shape=[jax.ShapeDtypeStruct((GROUP_W, n_seq), F32),
                   jax.ShapeDtypeStruct((N_HEADS, HEAD_DIM, HEAD_DIM, n_seq), F32)],
        compiler_params=_params(1),
    )(zt_s, zt_s, zt_s, zt_s, par_t, state_t)


def _sample_conv_kernel(z_ref, sc_ref, cc_ref, cpar_ref, cw_ref, oc_ref, od_ref, scn_ref, ccn_ref):
    nb = z_ref.shape[0]

    def zcol(cb):
        return z_ref[:, cb * GROUP_W:(cb + 1) * GROUP_W]

    u = zcol(COL_CC) * zcol(COL_CH)
    cb = zcol(COL_CB)
    ud = zcol(COL_DA) * _sigmoid(zcol(COL_DG))
    for i in range(nb):
        u_i = u[i:i + 1, :]
        conv_c = (cpar_ref[0:1, :] * sc_ref[i, 0:1, :] + cpar_ref[1:2, :] * sc_ref[i, 1:2, :]
                  + cpar_ref[2:3, :] * u_i)
        oc_ref[i:i + 1, :] = cb[i:i + 1, :] * conv_c
        scn_ref[i, 0:1, :] = sc_ref[i, 1:2, :]
        scn_ref[i, 1:2, :] = u_i

        ud_i = ud[i:i + 1, :]
        prev = cc_ref[i]
        conv_d = (jnp.sum(prev * cw_ref[0:CCONV_W - 1, :], axis=0, keepdims=True)
                  + cw_ref[CCONV_W - 1:CCONV_W, :] * ud_i + cpar_ref[3:4, :])
        y = _layer_norm(conv_d, cpar_ref[4:5, :], cpar_ref[5:6, :])
        od_ref[i:i + 1, :] = y * _sigmoid(y)
        ccn_ref[i, 0:CCONV_W - 2, :] = cc_ref[i, 1:CCONV_W - 1, :]
        ccn_ref[i, CCONV_W - 2:CCONV_W - 1, :] = ud_i


def _sample_conv(z_s, state_sconv, state_cconv, cpar, cw, layer):
    n_seq, n_cols = z_s.shape
    nb = SAMPLE_TILE
    const = lambda i: (0, 0)
    row_spec = pl.BlockSpec((nb, GROUP_W), lambda i: (i, 0))
    return pl.pallas_call(
        _sample_conv_kernel,
        grid=(n_seq // nb,),
        in_specs=[pl.BlockSpec((nb, n_cols), lambda i: (i, 0)),
                  pl.BlockSpec((nb, None, SCONV_W - 1, GROUP_W), lambda i: (i, layer, 0, 0)),
                  pl.BlockSpec((nb, None, CCONV_W - 1, GROUP_W), lambda i: (i, layer, 0, 0)),
                  pl.BlockSpec((8, GROUP_W), const),
                  pl.BlockSpec((CCONV_HIST, GROUP_W), const)],
        out_specs=[row_spec, row_spec,
                   pl.BlockSpec((nb, SCONV_W - 1, GROUP_W), lambda i: (i, 0, 0)),
                   pl.BlockSpec((nb, CCONV_W - 1, GROUP_W), lambda i: (i, 0, 0))],
        out_shape=[jax.ShapeDtypeStruct((n_seq, GROUP_W), F32)] * 2
                  + [jax.ShapeDtypeStruct((n_seq, SCONV_W - 1, GROUP_W), F32),
                     jax.ShapeDtypeStruct((n_seq, CCONV_W - 1, GROUP_W), F32)],
        compiler_params=_params(1),
    )(z_s, state_sconv, state_cconv, cpar, cw)


def _tiled_rows(ref, n_rows):
    return jnp.concatenate([ref[pl.ds(c, n_rows, stride=ROW_CHUNKS), :] for c in range(ROW_CHUNKS)], axis=1)


def _store_tiled_rows(ref, y):
    for c in range(ROW_CHUNKS):
        ref[pl.ds(c, y.shape[0], stride=ROW_CHUNKS), :] = y[:, c * 128:(c + 1) * 128]


def _outproj_ln_kernel(alpha, oa_ref, ob_ref, oc_ref, od_ref, w_ref, x_ref, g_ref, b_ref, xo_ref, xt_ref):
    mix = _dot(oa_ref[...], w_ref[0:GROUP_W, :])
    for j, ref in enumerate((ob_ref, oc_ref, od_ref), start=1):
        mix = mix + _dot(ref[...], w_ref[j * GROUP_W:(j + 1) * GROUP_W, :])
    y = _layer_norm(alpha * x_ref[...] + mix, g_ref[...], b_ref[...])
    xo_ref[...] = y
    _store_tiled_rows(xt_ref, y)


def _outproj_ln(oa, ob, oc, od, w, x, g, b, alpha):
    n_rows, d = x.shape
    assert d == ROW_CHUNKS * 128
    tm = _row_tile(n_rows, 1024, 128)
    part = pl.BlockSpec((tm, GROUP_W), lambda i: (i, 0))
    full = pl.BlockSpec((tm, d), lambda i: (i, 0))
    vec = pl.BlockSpec((1, d), lambda i: (0, 0))
    return pl.pallas_call(
        functools.partial(_outproj_ln_kernel, alpha),
        grid=(n_rows // tm,),
        in_specs=[part, part, part, part, pl.BlockSpec((4 * GROUP_W, d), lambda i: (0, 0)), full, vec, vec],
        out_specs=[full, pl.BlockSpec((tm * ROW_CHUNKS, 128), lambda i: (i, 0))],
        out_shape=[jax.ShapeDtypeStruct((n_rows, d), F32), jax.ShapeDtypeStruct((n_rows * ROW_CHUNKS, 128), F32)],
        compiler_params=_params(1),
    )(oa, ob, oc, od, w, x, g, b)


def _router_kernel(x_ref, w1_ref, w2_ref, w3_ref, b_ref, e_ref, g_ref):
    x1, x2, x3 = _split3(x_ref[...])
    w1, w2, w3 = w1_ref[...], w2_ref[...], w3_ref[...]
    logits = (_dot(x1, w1) + (_dot(x1, w2) + _dot(x2, w1)) + (_dot(x1, w3) + _dot(x2, w2) + _dot(x3, w1))
              + b_ref[...])
    n_lanes = logits.shape[1]
    lane = lax.broadcasted_iota(jnp.int32, (1, n_lanes), 1)
    e_out = jnp.zeros(logits.shape, jnp.int32)
    g_out = jnp.zeros(logits.shape, F32)
    denom = jnp.zeros((logits.shape[0], 1), F32)
    top = None
    for k in range(TOP_K):
        m = jnp.max(logits, axis=-1, keepdims=True)
        idx = jnp.min(jnp.where(logits == m, lane, n_lanes), axis=-1, keepdims=True)
        if top is None:
            top = m
        p = jnp.exp(m - top)
        denom = denom + p
        e_out = jnp.where(lane == k, idx, e_out)
        g_out = jnp.where(lane == k, p, g_out)
        logits = jnp.where(lane == idx, NEG_BIG * 2.0, logits)
    e_ref[...] = e_out
    g_ref[...] = g_out / denom


def _router(x, w_router, b_router):
    n_rows, d = x.shape
    n_exp = w_router.shape[1]
    lanes = 128
    wp = jnp.zeros((d, lanes), F32).at[:, :n_exp].set(w_router)
    bp = jnp.full((1, lanes), NEG_BIG, F32).at[0, :n_exp].set(b_router)
    w1, w2, w3 = _split3(wp)
    tm = _row_tile(n_rows, 1024, 128)
    wspec = pl.BlockSpec((d, lanes), lambda i: (0, 0))
    ospec = pl.BlockSpec((tm, lanes), lambda i: (i, 0))
    return pl.pallas_call(
        _router_kernel,
        grid=(n_rows // tm,),
        in_specs=[pl.BlockSpec((tm, d), lambda i: (i, 0)), wspec, wspec, wspec,
                  pl.BlockSpec((1, lanes), lambda i: (0, 0))],
        out_specs=[ospec, ospec],
        out_shape=[jax.ShapeDtypeStruct((n_rows, lanes), jnp.int32), jax.ShapeDtypeStruct((n_rows, lanes), F32)],
        compiler_params=_params(1),
    )(x, w1, w2, w3, bp)


def _expert_kernel(be_ref, nu_ref, idx_ref, x_hbm, wgu_ref, bgu_ref, wd_ref, bd_ref, y_hbm,
                   idx_smem, xbuf, ybuf, wgu_bf, wd_bf, sem_idx, sem_g, sem_s):
    i = pl.program_id(0)
    n = pl.num_programs(0)
    blk = xbuf.shape[1] // ROW_CHUNKS
    d_exp = wd_ref.shape[0]

    def idx_copy(block, s):
        return pltpu.make_async_copy(idx_ref.at[block], idx_smem.at[s], sem_idx.at[s])

    def gather_copy(s, j, r):
        return pltpu.make_async_copy(x_hbm.at[pl.ds(pl.multiple_of(r, ROW_CHUNKS), ROW_CHUNKS)],
                                     xbuf.at[s, pl.ds(j * ROW_CHUNKS, ROW_CHUNKS)], sem_g.at[s])

    def scatter_copy(s, j, r):
        return pltpu.make_async_copy(ybuf.at[s, pl.ds(j * ROW_CHUNKS, ROW_CHUNKS)],
                                     y_hbm.at[pl.ds(pl.multiple_of(r, ROW_CHUNKS), ROW_CHUNKS)], sem_s.at[s])

    def start_gather(s):
        for j in range(blk):
            gather_copy(s, j, idx_smem[s, j]).start()

    def start_scatter(s):
        for j in range(blk):
            scatter_copy(s, j, idx_smem[s, blk + j]).start()

    def wait_rows(copy, s):
        for j in range(blk):
            copy(s, j, 0).wait()

    @pl.when(i == 0)
    def _():
        idx_copy(0, 0).start()
        idx_copy(0, 0).wait()
        idx_copy(1, 1).start()
        start_gather(0)

    def step(s):
        o = 1 - s
        wait_rows(gather_copy, s)
        idx_copy(i + 1, o).wait()

        @pl.when(i >= 2)
        def _():
            wait_rows(scatter_copy, s)

        changed = jnp.logical_or(i == 0, be_ref[i] != be_ref[jnp.maximum(i - 1, 0)])

        @pl.when(changed)
        def _():
            wgu_bf[...] = wgu_ref[...].astype(BF16)
            wd_bf[...] = wd_ref[...].astype(BF16)

        start_gather(o)

        @pl.when(i < nu_ref[0])
        def _():
            xb = _tiled_rows(xbuf.at[s], blk).astype(BF16)
            gu = _dot(xb, wgu_bf[...]) + bgu_ref[...]
            gate = jnp.minimum(gu[:, :d_exp], SWIGLU_LIMIT)
            up = jnp.clip(gu[:, d_exp:], -SWIGLU_LIMIT, SWIGLU_LIMIT)
            act = (up + 1.0) * gate * _sigmoid(SWIGLU_ALPHA * gate)
            _store_tiled_rows(ybuf.at[s], _dot(act.astype(BF16), wd_bf[...]) + bd_ref[...])

        @pl.when(i >= nu_ref[0])
        def _():
            ybuf[s] = jnp.zeros(ybuf.shape[1:], F32)

        start_scatter(s)
        idx_copy(i + 2, s).start()

        @pl.when(i == n - 1)
        def _():
            wait_rows(scatter_copy, s)
            wait_rows(gather_copy, o)
            idx_copy(i + 2, s).wait()

            @pl.when(n >= 2)
            def _():
                wait_rows(scatter_copy, o)

    pl.when(i % 2 == 0)(lambda: step(0))
    pl.when(i % 2 == 1)(lambda: step(1))


def _experts(xt, idx, block_e, n_used, w_gate_up, b_gate_up, w_down, b_down, layer, n_slots):
    n_blocks = idx.shape[0] - 2
    blk = idx.shape[1] // 2
    d = w_gate_up.shape[2]
    d2 = w_gate_up.shape[3]
    grid_spec = pltpu.PrefetchScalarGridSpec(
        num_scalar_prefetch=2,
        grid=(n_blocks,),
        in_specs=[pl.BlockSpec(idx.shape, lambda i, be, nu: (0, 0)),
                  pl.BlockSpec(memory_space=pl.ANY),
                  pl.BlockSpec((None, None, d, d2), lambda i, be, nu: (layer, be[i], 0, 0)),
                  pl.BlockSpec((None, None, 1, d2), lambda i, be, nu: (layer, be[i], 0, 0)),
                  pl.BlockSpec((None, None, d2 // 2, d), lambda i, be, nu: (layer, be[i], 0, 0)),
                  pl.BlockSpec((None, None, 1, d), lambda i, be, nu: (layer, be[i], 0, 0))],
        out_specs=pl.BlockSpec(memory_space=pl.ANY),
        scratch_shapes=[pltpu.SMEM((2, 2 * blk), jnp.int32),
                        pltpu.VMEM((2, blk * ROW_CHUNKS, 128), F32),
                        pltpu.VMEM((2, blk * ROW_CHUNKS, 128), F32),
                        pltpu.VMEM((d, d2), BF16),
                        pltpu.VMEM((d2 // 2, d), BF16),
                        pltpu.SemaphoreType.DMA((2,)),
                        pltpu.SemaphoreType.DMA((2,)),
                        pltpu.SemaphoreType.DMA((2,))],
    )
    return pl.pallas_call(
        _expert_kernel,
        grid_spec=grid_spec,
        out_shape=jax.ShapeDtypeStruct((n_slots * ROW_CHUNKS, 128), F32),
        compiler_params=_params(0, 1),
    )(block_e, n_used, idx, xt, w_gate_up, b_gate_up, w_down, b_down)


def _route(top_e, n_rows, n_experts):
    blk = MOE_BLOCK
    n_assign = n_rows * TOP_K
    n_blocks = -(-n_assign // blk) + n_experts
    flat_e = top_e.reshape(-1)
    order = jnp.argsort(flat_e).astype(jnp.int32)
    sizes = jnp.sum(flat_e[:, None] == jnp.arange(n_experts, dtype=jnp.int32)[None, :], axis=0, dtype=jnp.int32)
    blocks_e = (sizes + blk - 1) // blk
    blk_end = jnp.cumsum(blocks_e)
    blk_start = blk_end - blocks_e
    grp_start = jnp.cumsum(sizes) - sizes
    block = jnp.arange(n_blocks, dtype=jnp.int32)
    block_e = jnp.minimum(jnp.sum(blk_end[None, :] <= block[:, None], axis=1, dtype=jnp.int32), n_experts - 1)
    lane = jnp.arange(blk, dtype=jnp.int32)[None, :]
    off = (block - blk_start[block_e])[:, None] * blk + lane
    valid = off < sizes[block_e][:, None]
    a = order[jnp.where(valid, grp_start[block_e][:, None] + off, 0)]
    tok = a // TOP_K
    src = jnp.where(valid, tok, 0)
    dump = n_assign + (block % 2)[:, None] * blk + lane
    dst = jnp.where(valid, (a % TOP_K) * n_rows + tok, dump)
    idx = jnp.concatenate([src, dst], axis=1) * ROW_CHUNKS
    idx = jnp.concatenate([idx, jnp.zeros((2, 2 * blk), jnp.int32)], axis=0)
    n_used = blk_end[-1:].astype(jnp.int32)
    return idx, block_e, n_used, n_assign + 2 * blk


def _combine_ln_kernel(alpha, x_ref, y0_ref, y1_ref, y2_ref, y3_ref, gate_ref, g_ref, b_ref, xo_ref, xb_ref):
    gates = gate_ref[...]
    tm = x_ref.shape[0]
    mix = gates[:, 0:1] * _tiled_rows(y0_ref, tm)
    for k, ref in enumerate((y1_ref, y2_ref, y3_ref), start=1):
        mix = mix + gates[:, k:k + 1] * _tiled_rows(ref, tm)
    y = _layer_norm(alpha * x_ref[...] + mix, g_ref[...], b_ref[...])
    xo_ref[...] = y
    xb_ref[...] = y.astype(BF16)


def _combine_ln(x, y_slots, gates, g, b, alpha):
    n_rows, d = x.shape
    tm = _row_tile(n_rows, 512, 64)
    nt = n_rows // tm
    full = pl.BlockSpec((tm, d), lambda i: (i, 0))
    vec = pl.BlockSpec((1, d), lambda i: (0, 0))

    def yspec(k):
        return pl.BlockSpec((tm * ROW_CHUNKS, 128), lambda i: (k * nt + i, 0))

    return pl.pallas_call(
        functools.partial(_combine_ln_kernel, alpha),
        grid=(nt,),
        in_specs=[full, yspec(0), yspec(1), yspec(2), yspec(3),
                  pl.BlockSpec((tm, gates.shape[1]), lambda i: (i, 0)), vec, vec],
        out_specs=[full, full],
        out_shape=[jax.ShapeDtypeStruct((n_rows, d), F32), jax.ShapeDtypeStruct((n_rows, d), BF16)],
        compiler_params=_params(1),
    )(x, y_slots, y_slots, y_slots, y_slots, gates, g, b)


def _pad_rows(a, n):
    return jnp.zeros((n, a.shape[1]), a.dtype).at[:a.shape[0]].set(a)


def kernel(x_prompt, x_sample, cache_sb_k, cache_sb_v, page_table, state_hgrn, state_sconv, state_cconv, meta_tokens, w_in, w_out, sb_bias, hg_lb_logits, hg_norm_w, sconv_w, cconv_w, cconv_b, cconv_ln_g, cconv_ln_b, ln1_g, ln1_b, w_router, b_router, w_gate_up, b_gate_up, w_down, b_down, ln2_g, ln2_b):
    n_batch, seq, d = x_prompt.shape
    n_seq = x_sample.shape[0]
    depth = w_in.shape[0]
    n_meta = meta_tokens.shape[0]
    n_experts = w_router.shape[2]
    t_len = n_meta + seq
    tp = -(-t_len // ATT_TILE) * ATT_TILE
    n_prompt_rows = n_batch * tp
    n_rows = n_prompt_rows + n_seq
    alpha = float((2 * depth) ** 0.25)
    assert d == 4 * GROUP_W and x_sample.shape[1] == 1 and n_seq % SAMPLE_TILE == 0

    pieces = []
    for b in range(n_batch):
        pieces += [meta_tokens.astype(F32), x_prompt[b], jnp.zeros((tp - t_len, d), F32)]
    pieces.append(x_sample.reshape(n_seq, d))
    x = jnp.concatenate(pieces, axis=0)
    xb = x.astype(BF16)

    lb_cum = jnp.cumsum(jax.nn.softmax(hg_lb_logits.astype(F32), axis=0), axis=0)
    lb_all = lb_cum - lb_cum[0]

    cache_kt = jnp.transpose(cache_sb_k, (0, 1, 3, 4, 2))
    cache_vt = jnp.transpose(cache_sb_v, (0, 1, 3, 4, 2))
    state_t = jnp.transpose(state_hgrn, (1, 2, 3, 4, 0))

    hg_p, hg_s, kp_l, vp_l, ks_l, vs_l, sc_p, sc_s, cc_p, cc_s = ([] for _ in range(10))
    for l in range(depth):
        lb = lb_all[l]
        hpar = _pad_rows(jnp.stack([jnp.log(lb), jnp.log1p(-lb), 1.0 - lb, hg_norm_w[l]]), 8)
        cpar = _pad_rows(jnp.concatenate([sconv_w[l], cconv_b[l][None], cconv_ln_g[l][None],
                                          cconv_ln_b[l][None]], axis=0), 8)
        cw = _pad_rows(cconv_w[l], CCONV_HIST)

        z, qkv = _inproj(xb, w_in[l].astype(BF16))

        oa_p, st = _hgrn_prompt(z, hpar, n_batch, tp, t_len)
        ob_p = _sbattn_prompt(qkv, sb_bias[l], n_batch, tp)
        oc_p, od_p, u_p, ud_p = _conv_prompt(z, cpar, cw, n_batch, tp)

        z_s = z[n_prompt_rows:]
        ob_s = _sample_attn(z_s.reshape(n_seq, 1, z.shape[1]), cache_kt, cache_vt, page_table, sb_bias[l], l)
        oa_t, s_new = _sample_hgrn(z_s.T, state_t, hpar.T, l)
        oa_s = oa_t.T
        oc_s, od_s, sc_new, cc_new = _sample_conv(z_s, state_sconv, state_cconv, cpar, cw, l)

        oa = jnp.concatenate([oa_p, oa_s.astype(BF16)], axis=0)
        ob = jnp.concatenate([ob_p, ob_s.astype(BF16)], axis=0)
        oc = jnp.concatenate([oc_p, oc_s.astype(BF16)], axis=0)
        od = jnp.concatenate([od_p, od_s.astype(BF16)], axis=0)
        x, xt = _outproj_ln(oa, ob, oc, od, w_out[l].astype(BF16), x, ln1_g[l][None], ln1_b[l][None], alpha)

        top_e, gates = _router(x, w_router[l], b_router[l])
        idx, block_e, n_used, n_slots = _route(top_e[:, :TOP_K], n_rows, n_experts)
        y_slots = _experts(xt, idx, block_e, n_used, w_gate_up, b_gate_up.reshape(depth, n_experts, 1, -1),
                           w_down, b_down.reshape(depth, n_experts, 1, -1), l, n_slots)
        x, xb = _combine_ln(x, y_slots, gates, ln2_g[l][None], ln2_b[l][None], alpha)

        zp = z[:n_prompt_rows].reshape(n_batch, tp, -1)
        st4 = st.reshape(n_batch, N_HEADS, HEAD_DIM, N_HEADS, HEAD_DIM)
        hg_p.append(jnp.stack([st4[:, h, :, h, :] for h in range(N_HEADS)], axis=1).swapaxes(-1, -2))
        hg_s.append(s_new)
        kp_l.append(zp[:, :t_len, COL_BK * GROUP_W:(COL_BK + 1) * GROUP_W].reshape(n_batch, t_len, N_HEADS, HEAD_DIM))
        vp_l.append(zp[:, :t_len, COL_BV * GROUP_W:(COL_BV + 1) * GROUP_W].reshape(n_batch, t_len, N_HEADS, HEAD_DIM))
        ks_l.append(z_s[:, COL_BK * GROUP_W:(COL_BK + 1) * GROUP_W].reshape(n_seq, 1, N_HEADS, HEAD_DIM))
        vs_l.append(z_s[:, COL_BV * GROUP_W:(COL_BV + 1) * GROUP_W].reshape(n_seq, 1, N_HEADS, HEAD_DIM))
        sc_p.append(u_p.reshape(n_batch, tp, GROUP_W)[:, t_len - (SCONV_W - 1):t_len])
        cc_p.append(ud_p.reshape(n_batch, tp, GROUP_W)[:, t_len - (CCONV_W - 1):t_len])
        sc_s.append(sc_new)
        cc_s.append(cc_new)

    y_prompt = x[:n_prompt_rows].reshape(n_batch, tp, d)[:, n_meta:t_len]
    y_sample = x[n_prompt_rows:].reshape(n_seq, 1, d)
    return (y_prompt, y_sample,
            jnp.stack(hg_p, axis=1), jnp.transpose(jnp.stack(hg_s, axis=0), (4, 0, 1, 2, 3)),
            jnp.stack(kp_l, axis=1), jnp.stack(vp_l, axis=1),
            jnp.stack(ks_l, axis=1), jnp.stack(vs_l, axis=1),
            jnp.stack(sc_p, axis=1), jnp.stack(sc_s, axis=1),
            jnp.stack(cc_p, axis=1), jnp.stack(cc_s, axis=1))
```

```python
import functools

import numpy as np
import jax
import jax.numpy as jnp
from jax import lax
from jax.experimental import pallas as pl
from jax.experimental.pallas import tpu as pltpu

F32 = jnp.float32
BF16 = jnp.bfloat16

HEAD_DIM = 64
N_HEADS = 4
GROUP_W = HEAD_DIM * N_HEADS
HG_CHUNK = 64
HG_MID = HG_CHUNK // 2 - 1
ATT_TILE = 256
ATT_ROWS = 256
TOP_K = 4
MOE_BLOCK = 256
SWIGLU_LIMIT = 7.0
SWIGLU_ALPHA = 1.702
LN_EPS = 1e-5
RMS_EPS = 1e-6
CCONV_W = 31
SCONV_W = 3
CCONV_HIST = 32
SCONV_HIST = 8
SAMPLE_TILE = 16
VMEM_LIMIT = 56 * 1024 * 1024
NEG_BIG = -1e30
LOG2E = 1.4426950408889634
ROW_CHUNKS = 8
DMA_THREADS = 2

COL_AQ, COL_AF, COL_AI, COL_AG, COL_BQ, COL_BK, COL_BV, COL_CB, COL_CC, COL_CH, COL_DA, COL_DG = range(12)


def _dot(a, b):
    return jnp.dot(a, b, preferred_element_type=F32)


def _dot_nt(a, b):
    return lax.dot_general(a, b, (((1,), (1,)), ((), ())), preferred_element_type=F32)


def _dot_tn(a, b):
    return lax.dot_general(a, b, (((0,), (0,)), ((), ())), preferred_element_type=F32)


def _split2(x):
    hi = x.astype(BF16)
    lo = (x - hi.astype(x.dtype)).astype(BF16)
    return hi, lo


def _split3(x):
    h1 = x.astype(BF16)
    r1 = x - h1.astype(x.dtype)
    h2 = r1.astype(BF16)
    h3 = (r1 - h2.astype(x.dtype)).astype(BF16)
    return h1, h2, h3


def _dot_exact_rhs(a_bf16, x):
    h1, h2, h3 = _split3(x)
    return _dot(a_bf16, h1) + _dot(a_bf16, h2) + _dot(a_bf16, h3)


def _dot_exact_lhs(x, a_bf16):
    h1, h2, h3 = _split3(x)
    return _dot(h1, a_bf16) + _dot(h2, a_bf16) + _dot(h3, a_bf16)


def _sigmoid(x):
    return 1.0 / (1.0 + jnp.exp(-x))


def _softplus(z):
    return jnp.maximum(z, 0.0) + jnp.log(1.0 + jnp.exp(-jnp.abs(z)))


def _layer_norm(y, g, b):
    mu = jnp.mean(y, axis=-1, keepdims=True)
    d = y - mu
    var = jnp.mean(d * d, axis=-1, keepdims=True)
    return d * lax.rsqrt(var + LN_EPS) * g + b


def _row_tile(n_rows, cap, mult):
    best = None
    for t in range(mult, cap + 1, mult):
        if n_rows % t == 0:
            best = t
    assert best is not None, (n_rows, cap, mult)
    return best


def _params(n_parallel, n_arbitrary=0):
    sem = ("parallel",) * n_parallel + ("arbitrary",) * n_arbitrary
    return pltpu.CompilerParams(dimension_semantics=sem, vmem_limit_bytes=VMEM_LIMIT)


def _inproj_kernel(x_ref, w_ref, z_ref, qkv_ref):
    z = _dot(x_ref[...], w_ref[...])
    z_ref[...] = z
    qkv_ref[...] = z[:, COL_BQ * GROUP_W:(COL_BV + 1) * GROUP_W].astype(BF16)


def _inproj(xb, w):
    n_rows, d = xb.shape
    n_cols = w.shape[1]
    tm = _row_tile(n_rows, 512, 64)
    return pl.pallas_call(
        _inproj_kernel,
        grid=(n_rows // tm,),
        in_specs=[pl.BlockSpec((tm, d), lambda i: (i, 0)),
                  pl.BlockSpec((d, n_cols), lambda i: (0, 0))],
        out_specs=[pl.BlockSpec((tm, n_cols), lambda i: (i, 0)),
                   pl.BlockSpec((tm, 3 * GROUP_W), lambda i: (i, 0))],
        out_shape=[jax.ShapeDtypeStruct((n_rows, n_cols), F32),
                   jax.ShapeDtypeStruct((n_rows, 3 * GROUP_W), BF16)],
        compiler_params=_params(1),
    )(xb, w)


def _hgrn_gates(fl, log_lb, log_1m_lb, one_m_lb):
    e = jnp.exp(-jnp.abs(fl))
    log_sig = jnp.minimum(fl, 0.0) - jnp.log1p(e)
    a = log_lb
    b = log_1m_lb + log_sig
    logf = jnp.maximum(a, b) + jnp.log1p(jnp.exp(-jnp.abs(a - b)))
    key = one_m_lb * (jnp.where(fl >= 0.0, e, 1.0) / (1.0 + e))
    return logf, key


def _hgrn_kernel(n_valid, q_ref, f_ref, i_ref, g_ref, par_ref, tri_ref, hones_ref, o_ref, st_ref, st_scr, o_scr):
    t = pl.program_id(1)
    tile = q_ref.shape[0]

    @pl.when(t == 0)
    def _():
        st_scr[...] = jnp.zeros_like(st_scr)

    logf, key = _hgrn_gates(f_ref[...], par_ref[0:1, :], par_ref[1:2, :], par_ref[2:3, :])
    row = t * tile + lax.broadcasted_iota(jnp.int32, (tile, 1), 0)
    valid = row < n_valid
    logf = jnp.where(valid, logf, 0.0)
    key = jnp.where(valid, key, 0.0)
    c = _dot_exact_rhs(tri_ref[...], logf)
    q = q_ref[...]
    v = i_ref[...]

    lane_head = lax.broadcasted_iota(jnp.int32, (1, GROUP_W), 1) // HEAD_DIM
    r_head = lax.broadcasted_iota(jnp.int32, (GROUP_W, 1), 0) // HEAD_DIM
    same_head = r_head == lane_head
    t_in = lax.broadcasted_iota(jnp.int32, (N_HEADS * HG_CHUNK, 1), 0) % HG_CHUNK
    s_in = lax.broadcasted_iota(jnp.int32, (1, HG_CHUNK), 1)
    causal = s_in <= t_in

    for j in range(tile // HG_CHUNK):
        r0 = j * HG_CHUNK
        cj = c[r0:r0 + HG_CHUNK]
        c_mid = cj[HG_MID:HG_MID + 1]
        c_last = cj[HG_CHUNK - 1:HG_CHUNK]
        qj = q[r0:r0 + HG_CHUNK]
        kj = key[r0:r0 + HG_CHUNK]
        vj = v[r0:r0 + HG_CHUNK].astype(BF16)
        q_mid = qj * jnp.exp(cj - c_mid)
        k_mid = (kj * jnp.exp(c_mid - cj)).astype(BF16)
        q_dec = (qj * jnp.exp(cj)).astype(BF16)
        k_dec = (kj * jnp.exp(c_last - cj)).astype(BF16)
        decay = jnp.exp(c_last)
        q_heads = jnp.concatenate([jnp.where(lane_head == h, q_mid, 0.0) for h in range(N_HEADS)], axis=0)
        scores = _dot_nt(q_heads.astype(BF16), k_mid)
        scores = jnp.where(causal, scores, 0.0).astype(BF16)
        o_heads = _dot(scores, vj)
        o_intra = jnp.where(lane_head == 0, o_heads[0:HG_CHUNK], 0.0)
        for h in range(1, N_HEADS):
            o_intra = o_intra + jnp.where(lane_head == h, o_heads[h * HG_CHUNK:(h + 1) * HG_CHUNK], 0.0)
        st = st_scr[...]
        o_inter = _dot_nt(q_dec, st.astype(BF16))
        o_scr[r0:r0 + HG_CHUNK, :] = o_intra + o_inter
        st_scr[...] = st * decay + jnp.where(same_head, _dot_tn(vj, k_dec), 0.0)

    o = o_scr[...]
    sq_hi, sq_lo = _split2(o * o)
    ms = (_dot(sq_hi, hones_ref[...]) + _dot(sq_lo, hones_ref[...])) * (1.0 / HEAD_DIM)
    on = o * lax.rsqrt(ms + RMS_EPS) * par_ref[3:4, :]
    g = g_ref[...]
    o_ref[...] = (on * (g * _sigmoid(g))).astype(BF16)

    @pl.when(t == pl.num_programs(1) - 1)
    def _():
        st_ref[0] = st_scr[...]


def _hgrn_prompt(z, par, n_batch, tp, n_valid):
    tile = ATT_TILE
    nt = tp // tile
    tri = np.zeros((tile, tile), np.float32)
    idx = np.arange(tile)
    tri[(idx[:, None] // HG_CHUNK == idx[None, :] // HG_CHUNK) & (idx[None, :] <= idx[:, None])] = 1.0
    hones = (idx[:, None] // HEAD_DIM == idx[None, :] // HEAD_DIM).astype(np.float32)

    def col(cb):
        return pl.BlockSpec((tile, GROUP_W), lambda b, t: (b * nt + t, cb))

    const = lambda b, t: (0, 0)
    return pl.pallas_call(
        functools.partial(_hgrn_kernel, n_valid),
        grid=(n_batch, nt),
        in_specs=[col(COL_AQ), col(COL_AF), col(COL_AI), col(COL_AG),
                  pl.BlockSpec((8, GROUP_W), const),
                  pl.BlockSpec((tile, tile), const),
                  pl.BlockSpec((GROUP_W, GROUP_W), const)],
        out_specs=[pl.BlockSpec((tile, GROUP_W), lambda b, t: (b * nt + t, 0)),
                   pl.BlockSpec((1, GROUP_W, GROUP_W), lambda b, t: (b, 0, 0))],
        out_shape=[jax.ShapeDtypeStruct((n_batch * tp, GROUP_W), BF16),
                   jax.ShapeDtypeStruct((n_batch, GROUP_W, GROUP_W), F32)],
        scratch_shapes=[pltpu.VMEM((GROUP_W, GROUP_W), F32), pltpu.VMEM((tile, GROUP_W), F32)],
        compiler_params=_params(1, 1),
    )(z, z, z, z, par, jnp.asarray(tri, BF16), jnp.asarray(hones, BF16))


def _sb_tile(z2, valid, carry, cum_ref):
    neg_abs = lax.bitcast_convert_type(lax.bitcast_convert_type(z2, jnp.uint32) | jnp.uint32(0x80000000), F32)
    sp = jnp.maximum(z2, 0.0) + jnp.log2(1.0 + jnp.exp2(neg_abs))
    if valid is not None:
        sp = jnp.where(valid, sp, 0.0)
    hi, lo = _split2(sp)
    cum = _dot(hi, cum_ref[...]) + _dot(lo, cum_ref[...])
    w = jnp.exp2(z2 - cum - carry)
    if valid is not None:
        w = jnp.where(valid, w, 0.0)
    return w, carry + cum[:, 0:1]


def _sbattn_kernel(ta_ref, tb_ref, kj_ref, pair_ref, bias_ref, qa_ref, qb_ref, k_ref, v_ref, cum_ref,
                   oa_ref, ob_ref, qm_scr, acc_scr, carry_scr):
    del pair_ref
    s = pl.program_id(1)
    ta = ta_ref[s]
    tb = tb_ref[s]
    kj = kj_ref[s]
    tile = qa_ref.shape[0]
    lane_head = lax.broadcasted_iota(jnp.int32, (1, GROUP_W), 1) // HEAD_DIM

    @pl.when(kj == tb)
    def _():
        acc_scr[...] = jnp.zeros_like(acc_scr)
        carry_scr[...] = jnp.zeros_like(carry_scr)
        q = jnp.concatenate([qa_ref[...], qb_ref[...]], axis=0)
        for h in range(N_HEADS):
            qm_scr[h] = jnp.where(lane_head == h, q * (LOG2E * HEAD_DIM ** -0.5), 0.0).astype(BF16)

    def body(masked):
        k = k_ref[...]
        v = v_ref[...]
        valid = None
        if masked:
            r = lax.broadcasted_iota(jnp.int32, (2 * tile, 1), 0)
            q_pos = jnp.where(r < tile, ta * tile + r, tb * tile + r - tile)
            valid = kj * tile + lax.broadcasted_iota(jnp.int32, (1, tile), 1) < q_pos
        pv = None
        for h in range(N_HEADS):
            z2 = _dot_nt(qm_scr[h], k) + bias_ref[h] * LOG2E
            w, carry = _sb_tile(z2, valid, carry_scr[h], cum_ref)
            carry_scr[h] = carry
            pv_h = _dot(w.astype(BF16), jnp.where(lane_head == h, v, jnp.zeros_like(v)))
            pv = pv_h if pv is None else pv + pv_h
        acc_scr[...] += pv

    pl.when(kj >= ta)(lambda: body(True))
    pl.when(kj < ta)(lambda: body(False))

    @pl.when(kj == 0)
    def _():
        oa_ref[...] = acc_scr[0:tile, :].astype(BF16)
        ob_ref[...] = acc_scr[tile:2 * tile, :].astype(BF16)


def _cum_matrix(n):
    idx = np.arange(n)
    return jnp.asarray((idx[:, None] >= idx[None, :]).astype(np.float32), BF16)


def _sbattn_prompt(qkv, bias, n_batch, tp):
    tile = ATT_TILE
    nq = tp // tile
    n_pairs = -(-nq // 2)
    ta_list, tb_list, kj_list, pair_list = [], [], [], []
    for p in range(n_pairs):
        ta, tb = 2 * p, min(2 * p + 1, nq - 1)
        for kj in range(tb, -1, -1):
            ta_list.append(ta)
            tb_list.append(tb)
            kj_list.append(kj)
            pair_list.append(p)
    as_arr = lambda v: jnp.asarray(np.array(v, np.int32))
    out_spec = pl.BlockSpec((tile, GROUP_W), lambda b, s, ta, tb, kj, pr, bs: (b * n_pairs + pr[s], 0))
    grid_spec = pltpu.PrefetchScalarGridSpec(
        num_scalar_prefetch=5,
        grid=(n_batch, len(kj_list)),
        in_specs=[pl.BlockSpec((tile, GROUP_W), lambda b, s, ta, tb, kj, pr, bs: (b * nq + ta[s], 0)),
                  pl.BlockSpec((tile, GROUP_W), lambda b, s, ta, tb, kj, pr, bs: (b * nq + tb[s], 0)),
                  pl.BlockSpec((tile, GROUP_W), lambda b, s, ta, tb, kj, pr, bs: (b * nq + kj[s], 1)),
                  pl.BlockSpec((tile, GROUP_W), lambda b, s, ta, tb, kj, pr, bs: (b * nq + kj[s], 2)),
                  pl.BlockSpec((tile, tile), lambda b, s, ta, tb, kj, pr, bs: (0, 0))],
        out_specs=[out_spec, out_spec],
        scratch_shapes=[pltpu.VMEM((N_HEADS, 2 * tile, GROUP_W), BF16),
                        pltpu.VMEM((2 * tile, GROUP_W), F32),
                        pltpu.VMEM((N_HEADS, 2 * tile, 1), F32)],
    )
    o_a, o_b = pl.pallas_call(
        _sbattn_kernel,
        grid_spec=grid_spec,
        out_shape=[jax.ShapeDtypeStruct((n_batch * n_pairs * tile, GROUP_W), BF16)] * 2,
        compiler_params=_params(1, 1),
    )(as_arr(ta_list), as_arr(tb_list), as_arr(kj_list), as_arr(pair_list), bias, qkv, qkv, qkv, qkv,
      _cum_matrix(tile))
    o = jnp.stack([o_a.reshape(n_batch, n_pairs, tile, GROUP_W), o_b.reshape(n_batch, n_pairs, tile, GROUP_W)],
                  axis=2)
    return o.reshape(n_batch, 2 * n_pairs * tile, GROUP_W)[:, :tp].reshape(n_batch * tp, GROUP_W)


def _conv_kernel(cb_ref, cc_ref, ch_ref, da_ref, dg_ref, par_ref, cw_ref, oc_ref, od_ref, u_ref, ud_ref,
                 ubuf, dbuf):
    t = pl.program_id(1)
    tile = cb_ref.shape[0]

    @pl.when(t == 0)
    def _():
        ubuf[0:SCONV_HIST, :] = jnp.zeros((SCONV_HIST, GROUP_W), F32)
        dbuf[0:CCONV_HIST, :] = jnp.zeros((CCONV_HIST, GROUP_W), F32)

    u = cc_ref[...] * ch_ref[...]
    ud = da_ref[...] * _sigmoid(dg_ref[...])
    u_ref[...] = u
    ud_ref[...] = ud
    ubuf[SCONV_HIST:SCONV_HIST + tile, :] = u
    dbuf[CCONV_HIST:CCONV_HIST + tile, :] = ud

    conv_c = par_ref[SCONV_W - 1:SCONV_W, :] * u
    for j in range(SCONV_W - 1):
        off = SCONV_HIST - (SCONV_W - 1) + j
        conv_c = conv_c + par_ref[j:j + 1, :] * ubuf[off:off + tile, :]
    oc_ref[...] = (cb_ref[...] * conv_c).astype(BF16)

    acc = cw_ref[CCONV_W - 1:CCONV_W, :] * ud + par_ref[3:4, :]
    for j in range(CCONV_W - 1):
        off = CCONV_HIST - (CCONV_W - 1) + j
        acc = acc + cw_ref[j:j + 1, :] * dbuf[off:off + tile, :]
    y = _layer_norm(acc, par_ref[4:5, :], par_ref[5:6, :])
    od_ref[...] = (y * _sigmoid(y)).astype(BF16)

    ubuf[0:SCONV_HIST, :] = ubuf[tile:tile + SCONV_HIST, :]
    dbuf[0:CCONV_HIST, :] = dbuf[tile:tile + CCONV_HIST, :]


def _conv_prompt(z, par, cw, n_batch, tp):
    tile = ATT_TILE
    nt = tp // tile

    def col(cb):
        return pl.BlockSpec((tile, GROUP_W), lambda b, t: (b * nt + t, cb))

    const = lambda b, t: (0, 0)
    out_spec = pl.BlockSpec((tile, GROUP_W), lambda b, t: (b * nt + t, 0))
    n_rows = n_batch * tp
    return pl.pallas_call(
        _conv_kernel,
        grid=(n_batch, nt),
        in_specs=[col(COL_CB), col(COL_CC), col(COL_CH), col(COL_DA), col(COL_DG),
                  pl.BlockSpec((8, GROUP_W), const), pl.BlockSpec((CCONV_HIST, GROUP_W), const)],
        out_specs=[out_spec, out_spec, out_spec, out_spec],
        out_shape=[jax.ShapeDtypeStruct((n_rows, GROUP_W), BF16), jax.ShapeDtypeStruct((n_rows, GROUP_W), BF16),
                   jax.ShapeDtypeStruct((n_rows, GROUP_W), F32), jax.ShapeDtypeStruct((n_rows, GROUP_W), F32)],
        scratch_shapes=[pltpu.VMEM((SCONV_HIST + tile, GROUP_W), F32),
                        pltpu.VMEM((CCONV_HIST + tile, GROUP_W), F32)],
        compiler_params=_params(1, 1),
    )(z, z, z, z, z, par, cw)


def _eye(n):
    return lax.broadcasted_iota(jnp.int32, (n, n), 0) == lax.broadcasted_iota(jnp.int32, (n, n), 1)


def _row_to_column(r):
    n = r.shape[1]
    return jnp.sum(jnp.where(_eye(n), jnp.broadcast_to(r, (n, n)), 0.0), axis=1, keepdims=True)


def _column_to_row(c):
    n = c.shape[0]
    return jnp.sum(jnp.where(_eye(n), jnp.broadcast_to(c, (n, n)), 0.0), axis=0, keepdims=True)


def _sample_attn_kernel(n_pages, pt_ref, bias_ref, z_ref, cum_ref, cross_ref, *refs):
    k_refs = refs[:n_pages]
    v_refs = refs[n_pages:2 * n_pages]
    o_ref = refs[2 * n_pages]
    page = k_refs[0].shape[2]
    rows = 8
    q = z_ref[:, COL_BQ * GROUP_W:(COL_BQ + 1) * GROUP_W] * (HEAD_DIM ** -0.5)
    q_cols = [jnp.broadcast_to(_row_to_column(q[:, h * HEAD_DIM:(h + 1) * HEAD_DIM]), (HEAD_DIM, page))
              for h in range(N_HEADS)]
    row = lax.broadcasted_iota(jnp.int32, (rows, 1), 0)
    tiles = []
    for p in range(n_pages):
        zp = jnp.zeros((rows, page), F32)
        for h in range(N_HEADS):
            zh = jnp.sum(q_cols[h] * k_refs[p][h], axis=0, keepdims=True) + bias_ref[h]
            zp = jnp.where(row == h, zh, zp)
        tiles.append(zp)
    z = jnp.concatenate(tiles, axis=0)
    sp = _softplus(z)
    hi, lo = _split2(sp)
    cum = _dot(hi, cum_ref[...]) + _dot(lo, cum_ref[...])
    carry = _dot_exact_rhs(cross_ref[...], jnp.broadcast_to(cum[:, 0:1], cum.shape))
    w = jnp.exp(z - cum - carry)
    outs = []
    for h in range(N_HEADS):
        acc = jnp.zeros((HEAD_DIM, page), F32)
        for p in range(n_pages):
            acc = acc + w[p * rows + h:p * rows + h + 1, :] * v_refs[p][h]
        outs.append(_column_to_row(jnp.sum(acc, axis=1, keepdims=True)))
    o_ref[...] = jnp.concatenate(outs, axis=1)


def _sample_attn(z3, cache_kt, cache_vt, page_table, bias, layer):
    n_seq = z3.shape[0]
    n_pages = page_table.shape[1]
    page = cache_kt.shape[4]
    n_cols = z3.shape[2]
    rows = 8
    idx = np.arange(n_pages * rows)
    cross = ((idx[:, None] % rows == idx[None, :] % rows) & (idx[None, :] // rows > idx[:, None] // rows))

    def page_spec(p):
        return pl.BlockSpec((None, None, N_HEADS, HEAD_DIM, page), lambda b, pt, bs: (pt[b, p], layer, 0, 0, 0))

    grid_spec = pltpu.PrefetchScalarGridSpec(
        num_scalar_prefetch=2,
        grid=(n_seq,),
        in_specs=[pl.BlockSpec((None, 1, n_cols), lambda b, pt, bs: (b, 0, 0)),
                  pl.BlockSpec((page, page), lambda b, pt, bs: (0, 0)),
                  pl.BlockSpec((n_pages * rows, n_pages * rows), lambda b, pt, bs: (0, 0))]
                 + [page_spec(p) for p in range(n_pages)] * 2,
        out_specs=pl.BlockSpec((None, 1, GROUP_W), lambda b, pt, bs: (b, 0, 0)),
    )
    out = pl.pallas_call(
        functools.partial(_sample_attn_kernel, n_pages),
        grid_spec=grid_spec,
        out_shape=jax.ShapeDtypeStruct((n_seq, 1, GROUP_W), F32),
        compiler_params=_params(1),
    )(page_table, bias, z3, _cum_matrix(page), jnp.asarray(cross.astype(np.float32), BF16),
      *([cache_kt] * n_pages), *([cache_vt] * n_pages))
    return out.reshape(n_seq, GROUP_W)


def _sample_hgrn_kernel(q_ref, f_ref, i_ref, g_ref, par_ref, s_ref, o_ref, sn_ref):
    logf, key = _hgrn_gates(f_ref[...], par_ref[:, 0:1], par_ref[:, 1:2], par_ref[:, 2:3])
    f = jnp.exp(logf)
    q = q_ref[...]
    v = i_ref[...]
    acc = jnp.zeros(v.shape, F32)
    for k in range(HEAD_DIM):
        s_new = f[k:k + 1, :] * s_ref[k] + key[k:k + 1, :] * v
        sn_ref[k] = s_new
        acc = acc + q[k:k + 1, :] * s_new
    ms = jnp.mean(acc * acc, axis=0, keepdims=True)
    g = g_ref[...]
    o_ref[...] = acc * lax.rsqrt(ms + RMS_EPS) * par_ref[:, 3:4] * (g * _sigmoid(g))


def _sample_hgrn(zt_s, state_t, par_t, layer):
    n_seq = zt_s.shape[1]

    def col(cb):
        return pl.BlockSpec((HEAD_DIM, n_seq), lambda h: (cb * N_HEADS + h, 0))

    return pl.pallas_call(
        _sample_hgrn_kernel,
        grid=(N_HEADS,),
        in_specs=[col(COL_AQ), col(COL_AF), col(COL_AI), col(COL_AG),
                  pl.BlockSpec((HEAD_DIM, 8), lambda h: (h, 0)),
                  pl.BlockSpec((None, None, HEAD_DIM, HEAD_DIM, n_seq), lambda h: (layer, h, 0, 0, 0))],
        out_specs=[pl.BlockSpec((HEAD_DIM, n_seq), lambda h: (h, 0)),
                   pl.BlockSpec((None, HEAD_DIM, HEAD_DIM, n_seq), lambda h: (h, 0, 0, 0))],
        out_shape=[jax.ShapeDtypeStruct((GROUP_W, n_seq), F32),
                   jax.ShapeDtypeStruct((N_HEADS, HEAD_DIM, HEAD_DIM, n_seq), F32)],
        compiler_params=_params(1),
    )(zt_s, zt_s, zt_s, zt_s, par_t, state_t)


def _sample_conv_kernel(z_ref, sc_ref, cc_ref, cpar_ref, cw_ref, oc_ref, od_ref, scn_ref, ccn_ref):
    nb = z_ref.shape[0]

    def zcol(cb):
        return z_ref[:, cb * GROUP_W:(cb + 1) * GROUP_W]

    u = zcol(COL_CC) * zcol(COL_CH)
    cb = zcol(COL_CB)
    ud = zcol(COL_DA) * _sigmoid(zcol(COL_DG))
    for i in range(nb):
        u_i = u[i:i + 1, :]
        conv_c = (cpar_ref[0:1, :] * sc_ref[i, 0:1, :] + cpar_ref[1:2, :] * sc_ref[i, 1:2, :]
                  + cpar_ref[2:3, :] * u_i)
        oc_ref[i:i + 1, :] = cb[i:i + 1, :] * conv_c
        scn_ref[i, 0:1, :] = sc_ref[i, 1:2, :]
        scn_ref[i, 1:2, :] = u_i

        ud_i = ud[i:i + 1, :]
        prev = cc_ref[i]
        conv_d = (jnp.sum(prev * cw_ref[0:CCONV_W - 1, :], axis=0, keepdims=True)
                  + cw_ref[CCONV_W - 1:CCONV_W, :] * ud_i + cpar_ref[3:4, :])
        y = _layer_norm(conv_d, cpar_ref[4:5, :], cpar_ref[5:6, :])
        od_ref[i:i + 1, :] = y * _sigmoid(y)
        ccn_ref[i, 0:CCONV_W - 2, :] = cc_ref[i, 1:CCONV_W - 1, :]
        ccn_ref[i, CCONV_W - 2:CCONV_W - 1, :] = ud_i


def _sample_conv(z_s, state_sconv, state_cconv, cpar, cw, layer):
    n_seq, n_cols = z_s.shape
    nb = SAMPLE_TILE
    const = lambda i: (0, 0)
    row_spec = pl.BlockSpec((nb, GROUP_W), lambda i: (i, 0))
    return pl.pallas_call(
        _sample_conv_kernel,
        grid=(n_seq // nb,),
        in_specs=[pl.BlockSpec((nb, n_cols), lambda i: (i, 0)),
                  pl.BlockSpec((nb, None, SCONV_W - 1, GROUP_W), lambda i: (i, layer, 0, 0)),
                  pl.BlockSpec((nb, None, CCONV_W - 1, GROUP_W), lambda i: (i, layer, 0, 0)),
                  pl.BlockSpec((8, GROUP_W), const),
                  pl.BlockSpec((CCONV_HIST, GROUP_W), const)],
        out_specs=[row_spec, row_spec,
                   pl.BlockSpec((nb, SCONV_W - 1, GROUP_W), lambda i: (i, 0, 0)),
                   pl.BlockSpec((nb, CCONV_W - 1, GROUP_W), lambda i: (i, 0, 0))],
        out_shape=[jax.ShapeDtypeStruct((n_seq, GROUP_W), F32)] * 2
                  + [jax.ShapeDtypeStruct((n_seq, SCONV_W - 1, GROUP_W), F32),
                     jax.ShapeDtypeStruct((n_seq, CCONV_W - 1, GROUP_W), F32)],
        compiler_params=_params(1),
    )(z_s, state_sconv, state_cconv, cpar, cw)


def _tiled_rows(ref, n_rows):
    return jnp.concatenate([ref[pl.ds(c, n_rows, stride=ROW_CHUNKS), :] for c in range(ROW_CHUNKS)], axis=1)


def _store_tiled_rows(ref, y):
    for c in range(ROW_CHUNKS):
        ref[pl.ds(c, y.shape[0], stride=ROW_CHUNKS), :] = y[:, c * 128:(c + 1) * 128]


def _outproj_ln_kernel(alpha, oa_ref, ob_ref, oc_ref, od_ref, w_ref, x_ref, g_ref, b_ref, xo_ref, xt_ref):
    mix = _dot(oa_ref[...], w_ref[0:GROUP_W, :])
    for j, ref in enumerate((ob_ref, oc_ref, od_ref), start=1):
        mix = mix + _dot(ref[...], w_ref[j * GROUP_W:(j + 1) * GROUP_W, :])
    y = _layer_norm(alpha * x_ref[...] + mix, g_ref[...], b_ref[...])
    xo_ref[...] = y
    _store_tiled_rows(xt_ref, y)


def _outproj_ln(oa, ob, oc, od, w, x, g, b, alpha):
    n_rows, d = x.shape
    assert d == ROW_CHUNKS * 128
    tm = _row_tile(n_rows, 1024, 128)
    part = pl.BlockSpec((tm, GROUP_W), lambda i: (i, 0))
    full = pl.BlockSpec((tm, d), lambda i: (i, 0))
    vec = pl.BlockSpec((1, d), lambda i: (0, 0))
    return pl.pallas_call(
        functools.partial(_outproj_ln_kernel, alpha),
        grid=(n_rows // tm,),
        in_specs=[part, part, part, part, pl.BlockSpec((4 * GROUP_W, d), lambda i: (0, 0)), full, vec, vec],
        out_specs=[full, pl.BlockSpec((tm * ROW_CHUNKS, 128), lambda i: (i, 0))],
        out_shape=[jax.ShapeDtypeStruct((n_rows, d), F32), jax.ShapeDtypeStruct((n_rows * ROW_CHUNKS, 128), F32)],
        compiler_params=_params(1),
    )(oa, ob, oc, od, w, x, g, b)


def _router_kernel(x_ref, w1_ref, w2_ref, w3_ref, b_ref, e_ref, g_ref):
    x1, x2, x3 = _split3(x_ref[...])
    w1, w2, w3 = w1_ref[...], w2_ref[...], w3_ref[...]
    logits = (_dot(x1, w1) + (_dot(x1, w2) + _dot(x2, w1)) + (_dot(x1, w3) + _dot(x2, w2) + _dot(x3, w1))
              + b_ref[...])
    n_lanes = logits.shape[1]
    lane = lax.broadcasted_iota(jnp.int32, (1, n_lanes), 1)
    e_out = jnp.zeros(logits.shape, jnp.int32)
    g_out = jnp.zeros(logits.shape, F32)
    denom = jnp.zeros((logits.shape[0], 1), F32)
    top = None
    for k in range(TOP_K):
        m = jnp.max(logits, axis=-1, keepdims=True)
        idx = jnp.min(jnp.where(logits == m, lane, n_lanes), axis=-1, keepdims=True)
        if top is None:
            top = m
        p = jnp.exp(m - top)
        denom = denom + p
        e_out = jnp.where(lane == k, idx, e_out)
        g_out = jnp.where(lane == k, p, g_out)
        logits = jnp.where(lane == idx, NEG_BIG * 2.0, logits)
    e_ref[...] = e_out
    g_ref[...] = g_out / denom


def _router(x, w_router, b_router):
    n_rows, d = x.shape
    n_exp = w_router.shape[1]
    lanes = 128
    wp = jnp.zeros((d, lanes), F32).at[:, :n_exp].set(w_router)
    bp = jnp.full((1, lanes), NEG_BIG, F32).at[0, :n_exp].set(b_router)
    w1, w2, w3 = _split3(wp)
    tm = _row_tile(n_rows, 1024, 128)
    wspec = pl.BlockSpec((d, lanes), lambda i: (0, 0))
    ospec = pl.BlockSpec((tm, lanes), lambda i: (i, 0))
    return pl.pallas_call(
        _router_kernel,
        grid=(n_rows // tm,),
        in_specs=[pl.BlockSpec((tm, d), lambda i: (i, 0)), wspec, wspec, wspec,
                  pl.BlockSpec((1, lanes), lambda i: (0, 0))],
        out_specs=[ospec, ospec],
        out_shape=[jax.ShapeDtypeStruct((n_rows, lanes), jnp.int32), jax.ShapeDtypeStruct((n_rows, lanes), F32)],
        compiler_params=_params(1),
    )(x, w1, w2, w3, bp)


def _expert_kernel(be_ref, nu_ref, idx_ref, x_hbm, wgu_ref, bgu_ref, wd_ref, bd_ref, y_hbm,
                   idx_smem, xbuf, ybuf, wgu_bf, wd_bf, sem_idx, sem_g, sem_s):
    i = pl.program_id(0)
    n = pl.num_programs(0)
    blk = xbuf.shape[1] // ROW_CHUNKS
    d_exp = wd_ref.shape[0]

    def idx_copy(block, s):
        return pltpu.make_async_copy(idx_ref.at[block], idx_smem.at[s], sem_idx.at[s])

    def gather_copy(s, j, r):
        return pltpu.make_async_copy(x_hbm.at[pl.ds(pl.multiple_of(r, ROW_CHUNKS), ROW_CHUNKS)],
                                     xbuf.at[s, pl.ds(j * ROW_CHUNKS, ROW_CHUNKS)], sem_g.at[s])

    def scatter_copy(s, j, r):
        return pltpu.make_async_copy(ybuf.at[s, pl.ds(j * ROW_CHUNKS, ROW_CHUNKS)],
                                     y_hbm.at[pl.ds(pl.multiple_of(r, ROW_CHUNKS), ROW_CHUNKS)], sem_s.at[s])

    def start_gather(s):
        for j in range(blk):
            gather_copy(s, j, idx_smem[s, j]).start(priority=j % DMA_THREADS)

    def start_scatter(s):
        for j in range(blk):
            scatter_copy(s, j, idx_smem[s, blk + j]).start(priority=j % DMA_THREADS)

    def wait_rows(copy, s):
        for j in range(blk):
            copy(s, j, 0).wait()

    @pl.when(i == 0)
    def _():
        idx_copy(0, 0).start()
        idx_copy(0, 0).wait()
        idx_copy(1, 1).start()
        start_gather(0)

    def step(s):
        o = 1 - s
        wait_rows(gather_copy, s)
        idx_copy(i + 1, o).wait()

        @pl.when(i >= 2)
        def _():
            wait_rows(scatter_copy, s)

        changed = jnp.logical_or(i == 0, be_ref[i] != be_ref[jnp.maximum(i - 1, 0)])

        @pl.when(changed)
        def _():
            wgu_bf[...] = wgu_ref[...].astype(BF16)
            wd_bf[...] = wd_ref[...].astype(BF16)

        start_gather(o)

        @pl.when(i < nu_ref[0])
        def _():
            xb = _tiled_rows(xbuf.at[s], blk).astype(BF16)
            gu = _dot(xb, wgu_bf[...]) + bgu_ref[...]
            gate = jnp.minimum(gu[:, :d_exp], SWIGLU_LIMIT)
            up = jnp.clip(gu[:, d_exp:], -SWIGLU_LIMIT, SWIGLU_LIMIT)
            act = (up + 1.0) * gate * _sigmoid(SWIGLU_ALPHA * gate)
            _store_tiled_rows(ybuf.at[s], _dot(act.astype(BF16), wd_bf[...]) + bd_ref[...])

        @pl.when(i >= nu_ref[0])
        def _():
            ybuf[s] = jnp.zeros(ybuf.shape[1:], F32)

        start_scatter(s)
        idx_copy(i + 2, s).start()

        @pl.when(i == n - 1)
        def _():
            wait_rows(scatter_copy, s)
            wait_rows(gather_copy, o)
            idx_copy(i + 2, s).wait()

            @pl.when(n >= 2)
            def _():
                wait_rows(scatter_copy, o)

    pl.when(i % 2 == 0)(lambda: step(0))
    pl.when(i % 2 == 1)(lambda: step(1))


def _experts(xt, idx, block_e, n_used, w_gate_up, b_gate_up, w_down, b_down, layer, n_slots):
    n_blocks = idx.shape[0] - 2
    blk = idx.shape[1] // 2
    d = w_gate_up.shape[2]
    d2 = w_gate_up.shape[3]
    grid_spec = pltpu.PrefetchScalarGridSpec(
        num_scalar_prefetch=2,
        grid=(n_blocks,),
        in_specs=[pl.BlockSpec(idx.shape, lambda i, be, nu: (0, 0)),
                  pl.BlockSpec(memory_space=pl.ANY),
                  pl.BlockSpec((None, None, d, d2), lambda i, be, nu: (layer, be[i], 0, 0)),
                  pl.BlockSpec((None, None, 1, d2), lambda i, be, nu: (layer, be[i], 0, 0)),
                  pl.BlockSpec((None, None, d2 // 2, d), lambda i, be, nu: (layer, be[i], 0, 0)),
                  pl.BlockSpec((None, None, 1, d), lambda i, be, nu: (layer, be[i], 0, 0))],
        out_specs=pl.BlockSpec(memory_space=pl.ANY),
        scratch_shapes=[pltpu.SMEM((2, 2 * blk), jnp.int32),
                        pltpu.VMEM((2, blk * ROW_CHUNKS, 128), F32),
                        pltpu.VMEM((2, blk * ROW_CHUNKS, 128), F32),
                        pltpu.VMEM((d, d2), BF16),
                        pltpu.VMEM((d2 // 2, d), BF16),
                        pltpu.SemaphoreType.DMA((2,)),
                        pltpu.SemaphoreType.DMA((2,)),
                        pltpu.SemaphoreType.DMA((2,))],
    )
    return pl.pallas_call(
        _expert_kernel,
        grid_spec=grid_spec,
        out_shape=jax.ShapeDtypeStruct((n_slots * ROW_CHUNKS, 128), F32),
        compiler_params=_params(0, 1),
    )(block_e, n_used, idx, xt, w_gate_up, b_gate_up, w_down, b_down)


def _route(top_e, n_rows, n_experts):
    blk = MOE_BLOCK
    n_assign = n_rows * TOP_K
    n_blocks = -(-n_assign // blk) + n_experts
    flat_e = top_e.reshape(-1)
    a_bits = n_assign.bit_length()
    assert (n_experts << a_bits) < 2 ** 31
    packed = jnp.sort(flat_e * (1 << a_bits) + jnp.arange(n_assign, dtype=jnp.int32))
    order = packed & ((1 << a_bits) - 1)
    sizes = jnp.sum(flat_e[:, None] == jnp.arange(n_experts, dtype=jnp.int32)[None, :], axis=0, dtype=jnp.int32)
    blocks_e = (sizes + blk - 1) // blk
    blk_end = jnp.cumsum(blocks_e)
    blk_start = blk_end - blocks_e
    grp_start = jnp.cumsum(sizes) - sizes
    block = jnp.arange(n_blocks, dtype=jnp.int32)
    block_e = jnp.minimum(jnp.sum(blk_end[None, :] <= block[:, None], axis=1, dtype=jnp.int32), n_experts - 1)
    lane = jnp.arange(blk, dtype=jnp.int32)[None, :]
    off = (block - blk_start[block_e])[:, None] * blk + lane
    valid = off < sizes[block_e][:, None]
    a = order[jnp.where(valid, grp_start[block_e][:, None] + off, 0)]
    tok = a // TOP_K
    src = jnp.where(valid, tok, 0)
    dump = n_assign + (block % 2)[:, None] * blk + lane
    dst = jnp.where(valid, (a % TOP_K) * n_rows + tok, dump)
    idx = jnp.concatenate([src, dst], axis=1) * ROW_CHUNKS
    idx = jnp.concatenate([idx, jnp.zeros((2, 2 * blk), jnp.int32)], axis=0)
    n_used = blk_end[-1:].astype(jnp.int32)
    return idx, block_e, n_used, n_assign + 2 * blk


def _combine_ln_kernel(alpha, x_ref, y0_ref, y1_ref, y2_ref, y3_ref, gate_ref, g_ref, b_ref, xo_ref, xb_ref):
    gates = gate_ref[...]
    tm = x_ref.shape[0]
    mix = gates[:, 0:1] * _tiled_rows(y0_ref, tm)
    for k, ref in enumerate((y1_ref, y2_ref, y3_ref), start=1):
        mix = mix + gates[:, k:k + 1] * _tiled_rows(ref, tm)
    y = _layer_norm(alpha * x_ref[...] + mix, g_ref[...], b_ref[...])
    xo_ref[...] = y
    xb_ref[...] = y.astype(BF16)


def _combine_ln(x, y_slots, gates, g, b, alpha):
    n_rows, d = x.shape
    tm = _row_tile(n_rows, 512, 64)
    nt = n_rows // tm
    full = pl.BlockSpec((tm, d), lambda i: (i, 0))
    vec = pl.BlockSpec((1, d), lambda i: (0, 0))

    def yspec(k):
        return pl.BlockSpec((tm * ROW_CHUNKS, 128), lambda i: (k * nt + i, 0))

    return pl.pallas_call(
        functools.partial(_combine_ln_kernel, alpha),
        grid=(nt,),
        in_specs=[full, yspec(0), yspec(1), yspec(2), yspec(3),
                  pl.BlockSpec((tm, gates.shape[1]), lambda i: (i, 0)), vec, vec],
        out_specs=[full, full],
        out_shape=[jax.ShapeDtypeStruct((n_rows, d), F32), jax.ShapeDtypeStruct((n_rows, d), BF16)],
        compiler_params=_params(1),
    )(x, y_slots, y_slots, y_slots, y_slots, gates, g, b)


def _pad_rows(a, n):
    return jnp.zeros((n, a.shape[1]), a.dtype).at[:a.shape[0]].set(a)


def kernel(x_prompt, x_sample, cache_sb_k, cache_sb_v, page_table, state_hgrn, state_sconv, state_cconv, meta_tokens, w_in, w_out, sb_bias, hg_lb_logits, hg_norm_w, sconv_w, cconv_w, cconv_b, cconv_ln_g, cconv_ln_b, ln1_g, ln1_b, w_router, b_router, w_gate_up, b_gate_up, w_down, b_down, ln2_g, ln2_b):
    n_batch, seq, d = x_prompt.shape
    n_seq = x_sample.shape[0]
    depth = w_in.shape[0]
    n_meta = meta_tokens.shape[0]
    n_experts = w_router.shape[2]
    t_len = n_meta + seq
    tp = -(-t_len // ATT_TILE) * ATT_TILE
    n_prompt_rows = n_batch * tp
    n_rows = n_prompt_rows + n_seq
    alpha = float((2 * depth) ** 0.25)
    assert d == 4 * GROUP_W and x_sample.shape[1] == 1 and n_seq % SAMPLE_TILE == 0

    pieces = []
    for b in range(n_batch):
        pieces += [meta_tokens.astype(F32), x_prompt[b], jnp.zeros((tp - t_len, d), F32)]
    pieces.append(x_sample.reshape(n_seq, d))
    x = jnp.concatenate(pieces, axis=0)
    xb = x.astype(BF16)

    lb_cum = jnp.cumsum(jax.nn.softmax(hg_lb_logits.astype(F32), axis=0), axis=0)
    lb_all = lb_cum - lb_cum[0]

    cache_kt = jnp.transpose(cache_sb_k, (0, 1, 3, 4, 2))
    cache_vt = jnp.transpose(cache_sb_v, (0, 1, 3, 4, 2))
    state_t = jnp.transpose(state_hgrn, (1, 2, 3, 4, 0))

    hg_p, hg_s, kp_l, vp_l, ks_l, vs_l, sc_p, sc_s, cc_p, cc_s = ([] for _ in range(10))
    for l in range(depth):
        lb = lb_all[l]
        hpar = _pad_rows(jnp.stack([jnp.log(lb), jnp.log1p(-lb), 1.0 - lb, hg_norm_w[l]]), 8)
        cpar = _pad_rows(jnp.concatenate([sconv_w[l], cconv_b[l][None], cconv_ln_g[l][None],
                                          cconv_ln_b[l][None]], axis=0), 8)
        cw = _pad_rows(cconv_w[l], CCONV_HIST)

        z, qkv = _inproj(xb, w_in[l].astype(BF16))

        oa_p, st = _hgrn_prompt(z, hpar, n_batch, tp, t_len)
        ob_p = _sbattn_prompt(qkv, sb_bias[l], n_batch, tp)
        oc_p, od_p, u_p, ud_p = _conv_prompt(z, cpar, cw, n_batch, tp)

        z_s = z[n_prompt_rows:]
        ob_s = _sample_attn(z_s.reshape(n_seq, 1, z.shape[1]), cache_kt, cache_vt, page_table, sb_bias[l], l)
        oa_t, s_new = _sample_hgrn(z_s.T, state_t, hpar.T, l)
        oa_s = oa_t.T
        oc_s, od_s, sc_new, cc_new = _sample_conv(z_s, state_sconv, state_cconv, cpar, cw, l)

        oa = jnp.concatenate([oa_p, oa_s.astype(BF16)], axis=0)
        ob = jnp.concatenate([ob_p, ob_s.astype(BF16)], axis=0)
        oc = jnp.concatenate([oc_p, oc_s.astype(BF16)], axis=0)
        od = jnp.concatenate([od_p, od_s.astype(BF16)], axis=0)
        x, xt = _outproj_ln(oa, ob, oc, od, w_out[l].astype(BF16), x, ln1_g[l][None], ln1_b[l][None], alpha)

        top_e, gates = _router(x, w_router[l], b_router[l])
        idx, block_e, n_used, n_slots = _route(top_e[:, :TOP_K], n_rows, n_experts)
        y_slots = _experts(xt, idx, block_e, n_used, w_gate_up, b_gate_up.reshape(depth, n_experts, 1, -1),
                           w_down, b_down.reshape(depth, n_experts, 1, -1), l, n_slots)
        x, xb = _combine_ln(x, y_slots, gates, ln2_g[l][None], ln2_b[l][None], alpha)

        def prompt_heads(cb):
            cols = slice(cb * GROUP_W, (cb + 1) * GROUP_W)
            return jnp.stack([z[b * tp:b * tp + t_len, cols] for b in range(n_batch)], axis=0).reshape(
                n_batch, t_len, N_HEADS, HEAD_DIM)

        st4 = st.reshape(n_batch, N_HEADS, HEAD_DIM, N_HEADS, HEAD_DIM)
        hg_p.append(jnp.stack([st4[:, h, :, h, :] for h in range(N_HEADS)], axis=1).swapaxes(-1, -2))
        hg_s.append(s_new)
        kp_l.append(prompt_heads(COL_BK))
        vp_l.append(prompt_heads(COL_BV))
        ks_l.append(z_s[:, COL_BK * GROUP_W:(COL_BK + 1) * GROUP_W].reshape(n_seq, 1, N_HEADS, HEAD_DIM))
        vs_l.append(z_s[:, COL_BV * GROUP_W:(COL_BV + 1) * GROUP_W].reshape(n_seq, 1, N_HEADS, HEAD_DIM))
        sc_p.append(u_p.reshape(n_batch, tp, GROUP_W)[:, t_len - (SCONV_W - 1):t_len])
        cc_p.append(ud_p.reshape(n_batch, tp, GROUP_W)[:, t_len - (CCONV_W - 1):t_len])
        sc_s.append(sc_new)
        cc_s.append(cc_new)

    y_prompt = x[:n_prompt_rows].reshape(n_batch, tp, d)[:, n_meta:t_len]
    y_sample = x[n_prompt_rows:].reshape(n_seq, 1, d)
    return (y_prompt, y_sample,
            jnp.stack(hg_p, axis=1), jnp.transpose(jnp.stack(hg_s, axis=0), (4, 0, 1, 2, 3)),
            jnp.stack(kp_l, axis=1), jnp.stack(vp_l, axis=1),
            jnp.stack(ks_l, axis=1), jnp.stack(vs_l, axis=1),
            jnp.stack(sc_p, axis=1), jnp.stack(sc_s, axis=1),
            jnp.stack(cc_p, axis=1), jnp.stack(cc_s, axis=1))
```

```python
import functools

import numpy as np
import jax
import jax.numpy as jnp
from jax import lax
from jax.experimental import pallas as pl
from jax.experimental.pallas import tpu as pltpu

F32 = jnp.float32
BF16 = jnp.bfloat16

HEAD_DIM = 64
N_HEADS = 4
GROUP_W = HEAD_DIM * N_HEADS
HG_CHUNK = 64
HG_MID = HG_CHUNK // 2 - 1
ATT_TILE = 256
ATT_ROWS = 256
TOP_K = 4
MOE_BLOCK = 256
SWIGLU_LIMIT = 7.0
SWIGLU_ALPHA = 1.702
LN_EPS = 1e-5
RMS_EPS = 1e-6
CCONV_W = 31
SCONV_W = 3
CCONV_HIST = 32
SCONV_HIST = 8
SAMPLE_TILE = 16
VMEM_LIMIT = 56 * 1024 * 1024
NEG_BIG = -1e30
LOG2E = 1.4426950408889634
ROW_CHUNKS = 8
DMA_THREADS = 2

COL_AQ, COL_AF, COL_AI, COL_AG, COL_BQ, COL_BK, COL_BV, COL_CB, COL_CC, COL_CH, COL_DA, COL_DG = range(12)


def _dot(a, b):
    return jnp.dot(a, b, preferred_element_type=F32)


def _dot_nt(a, b):
    return lax.dot_general(a, b, (((1,), (1,)), ((), ())), preferred_element_type=F32)


def _dot_tn(a, b):
    return lax.dot_general(a, b, (((0,), (0,)), ((), ())), preferred_element_type=F32)


def _split2(x):
    hi = x.astype(BF16)
    lo = (x - hi.astype(x.dtype)).astype(BF16)
    return hi, lo


def _split3(x):
    h1 = x.astype(BF16)
    r1 = x - h1.astype(x.dtype)
    h2 = r1.astype(BF16)
    h3 = (r1 - h2.astype(x.dtype)).astype(BF16)
    return h1, h2, h3


def _dot_exact_rhs(a_bf16, x):
    h1, h2, h3 = _split3(x)
    return _dot(a_bf16, h1) + _dot(a_bf16, h2) + _dot(a_bf16, h3)


def _dot_exact_lhs(x, a_bf16):
    h1, h2, h3 = _split3(x)
    return _dot(h1, a_bf16) + _dot(h2, a_bf16) + _dot(h3, a_bf16)


def _sigmoid(x):
    return 1.0 / (1.0 + jnp.exp(-x))


def _softplus(z):
    return jnp.maximum(z, 0.0) + jnp.log(1.0 + jnp.exp(-jnp.abs(z)))


def _layer_norm(y, g, b):
    mu = jnp.mean(y, axis=-1, keepdims=True)
    d = y - mu
    var = jnp.mean(d * d, axis=-1, keepdims=True)
    return d * lax.rsqrt(var + LN_EPS) * g + b


def _row_tile(n_rows, cap, mult):
    best = None
    for t in range(mult, cap + 1, mult):
        if n_rows % t == 0:
            best = t
    assert best is not None, (n_rows, cap, mult)
    return best


def _params(n_parallel, n_arbitrary=0):
    sem = ("parallel",) * n_parallel + ("arbitrary",) * n_arbitrary
    return pltpu.CompilerParams(dimension_semantics=sem, vmem_limit_bytes=VMEM_LIMIT)


def _inproj_kernel(x_ref, w_ref, z_ref, qkv_ref):
    z = _dot(x_ref[...], w_ref[...])
    z_ref[...] = z
    qkv_ref[...] = z[:, COL_BQ * GROUP_W:(COL_BV + 1) * GROUP_W].astype(BF16)


def _inproj(xb, w):
    n_rows, d = xb.shape
    n_cols = w.shape[1]
    tm = _row_tile(n_rows, 512, 64)
    return pl.pallas_call(
        _inproj_kernel,
        grid=(n_rows // tm,),
        in_specs=[pl.BlockSpec((tm, d), lambda i: (i, 0)),
                  pl.BlockSpec((d, n_cols), lambda i: (0, 0))],
        out_specs=[pl.BlockSpec((tm, n_cols), lambda i: (i, 0)),
                   pl.BlockSpec((tm, 3 * GROUP_W), lambda i: (i, 0))],
        out_shape=[jax.ShapeDtypeStruct((n_rows, n_cols), F32),
                   jax.ShapeDtypeStruct((n_rows, 3 * GROUP_W), BF16)],
        compiler_params=_params(1),
    )(xb, w)


def _hgrn_gates(fl, log_lb, log_1m_lb, one_m_lb):
    e = jnp.exp(-jnp.abs(fl))
    log_sig = jnp.minimum(fl, 0.0) - jnp.log1p(e)
    a = log_lb
    b = log_1m_lb + log_sig
    logf = jnp.maximum(a, b) + jnp.log1p(jnp.exp(-jnp.abs(a - b)))
    key = one_m_lb * (jnp.where(fl >= 0.0, e, 1.0) / (1.0 + e))
    return logf, key


def _hgrn_kernel(n_valid, q_ref, f_ref, i_ref, g_ref, par_ref, tri_ref, hones_ref, o_ref, st_ref, st_scr, o_scr):
    t = pl.program_id(1)
    tile = q_ref.shape[0]

    @pl.when(t == 0)
    def _():
        st_scr[...] = jnp.zeros_like(st_scr)

    logf, key = _hgrn_gates(f_ref[...], par_ref[0:1, :], par_ref[1:2, :], par_ref[2:3, :])
    row = t * tile + lax.broadcasted_iota(jnp.int32, (tile, 1), 0)
    valid = row < n_valid
    logf = jnp.where(valid, logf, 0.0)
    key = jnp.where(valid, key, 0.0)
    c = _dot_exact_rhs(tri_ref[...], logf)
    q = q_ref[...]
    v = i_ref[...]

    lane_head = lax.broadcasted_iota(jnp.int32, (1, GROUP_W), 1) // HEAD_DIM
    r_head = lax.broadcasted_iota(jnp.int32, (GROUP_W, 1), 0) // HEAD_DIM
    same_head = r_head == lane_head
    t_in = lax.broadcasted_iota(jnp.int32, (N_HEADS * HG_CHUNK, 1), 0) % HG_CHUNK
    s_in = lax.broadcasted_iota(jnp.int32, (1, HG_CHUNK), 1)
    causal = s_in <= t_in

    for j in range(tile // HG_CHUNK):
        r0 = j * HG_CHUNK
        cj = c[r0:r0 + HG_CHUNK]
        c_mid = cj[HG_MID:HG_MID + 1]
        c_last = cj[HG_CHUNK - 1:HG_CHUNK]
        qj = q[r0:r0 + HG_CHUNK]
        kj = key[r0:r0 + HG_CHUNK]
        vj = v[r0:r0 + HG_CHUNK].astype(BF16)
        q_mid = qj * jnp.exp(cj - c_mid)
        k_mid = (kj * jnp.exp(c_mid - cj)).astype(BF16)
        q_dec = (qj * jnp.exp(cj)).astype(BF16)
        k_dec = (kj * jnp.exp(c_last - cj)).astype(BF16)
        decay = jnp.exp(c_last)
        q_heads = jnp.concatenate([jnp.where(lane_head == h, q_mid, 0.0) for h in range(N_HEADS)], axis=0)
        scores = _dot_nt(q_heads.astype(BF16), k_mid)
        scores = jnp.where(causal, scores, 0.0).astype(BF16)
        o_heads = _dot(scores, vj)
        o_intra = jnp.where(lane_head == 0, o_heads[0:HG_CHUNK], 0.0)
        for h in range(1, N_HEADS):
            o_intra = o_intra + jnp.where(lane_head == h, o_heads[h * HG_CHUNK:(h + 1) * HG_CHUNK], 0.0)
        st = st_scr[...]
        o_inter = _dot_nt(q_dec, st.astype(BF16))
        o_scr[r0:r0 + HG_CHUNK, :] = o_intra + o_inter
        st_scr[...] = st * decay + jnp.where(same_head, _dot_tn(vj, k_dec), 0.0)

    o = o_scr[...]
    sq_hi, sq_lo = _split2(o * o)
    ms = (_dot(sq_hi, hones_ref[...]) + _dot(sq_lo, hones_ref[...])) * (1.0 / HEAD_DIM)
    on = o * lax.rsqrt(ms + RMS_EPS) * par_ref[3:4, :]
    g = g_ref[...]
    o_ref[...] = (on * (g * _sigmoid(g))).astype(BF16)

    @pl.when(t == pl.num_programs(1) - 1)
    def _():
        st_ref[0] = st_scr[...]


def _hgrn_prompt(z, par, n_batch, tp, n_valid):
    tile = ATT_TILE
    nt = tp // tile
    tri = np.zeros((tile, tile), np.float32)
    idx = np.arange(tile)
    tri[(idx[:, None] // HG_CHUNK == idx[None, :] // HG_CHUNK) & (idx[None, :] <= idx[:, None])] = 1.0
    hones = (idx[:, None] // HEAD_DIM == idx[None, :] // HEAD_DIM).astype(np.float32)

    def col(cb):
        return pl.BlockSpec((tile, GROUP_W), lambda b, t: (b * nt + t, cb))

    const = lambda b, t: (0, 0)
    return pl.pallas_call(
        functools.partial(_hgrn_kernel, n_valid),
        grid=(n_batch, nt),
        in_specs=[col(COL_AQ), col(COL_AF), col(COL_AI), col(COL_AG),
                  pl.BlockSpec((8, GROUP_W), const),
                  pl.BlockSpec((tile, tile), const),
                  pl.BlockSpec((GROUP_W, GROUP_W), const)],
        out_specs=[pl.BlockSpec((tile, GROUP_W), lambda b, t: (b * nt + t, 0)),
                   pl.BlockSpec((1, GROUP_W, GROUP_W), lambda b, t: (b, 0, 0))],
        out_shape=[jax.ShapeDtypeStruct((n_batch * tp, GROUP_W), BF16),
                   jax.ShapeDtypeStruct((n_batch, GROUP_W, GROUP_W), F32)],
        scratch_shapes=[pltpu.VMEM((GROUP_W, GROUP_W), F32), pltpu.VMEM((tile, GROUP_W), F32)],
        compiler_params=_params(1, 1),
    )(z, z, z, z, par, jnp.asarray(tri, BF16), jnp.asarray(hones, BF16))


def _sb_tile(z2, valid, carry, cum_ref):
    neg_abs = lax.bitcast_convert_type(lax.bitcast_convert_type(z2, jnp.uint32) | jnp.uint32(0x80000000), F32)
    sp = jnp.maximum(z2, 0.0) + jnp.log2(1.0 + jnp.exp2(neg_abs))
    if valid is not None:
        sp = jnp.where(valid, sp, 0.0)
    hi, lo = _split2(sp)
    cum = _dot(hi, cum_ref[...]) + _dot(lo, cum_ref[...])
    w = jnp.exp2(z2 - cum - carry)
    if valid is not None:
        w = jnp.where(valid, w, 0.0)
    return w, carry + cum[:, 0:1]


def _sbattn_kernel(ta_ref, tb_ref, kj_ref, pair_ref, bias_ref, qa_ref, qb_ref, k1_ref, v1_ref, k2_ref, v2_ref,
                   cum_ref, oa_ref, ob_ref, qm_scr, acc_scr, carry_scr):
    del pair_ref
    s = pl.program_id(1)
    ta = ta_ref[s]
    tb = tb_ref[s]
    kj = kj_ref[s]
    kj2 = kj - 1
    tile = qa_ref.shape[0]
    lane_head = lax.broadcasted_iota(jnp.int32, (1, GROUP_W), 1) // HEAD_DIM

    @pl.when(kj == tb)
    def _():
        acc_scr[...] = jnp.zeros_like(acc_scr)
        carry_scr[...] = jnp.zeros_like(carry_scr)
        q = jnp.concatenate([qa_ref[...], qb_ref[...]], axis=0)
        for h in range(N_HEADS):
            qm_scr[h] = jnp.where(lane_head == h, q * (LOG2E * HEAD_DIM ** -0.5), 0.0).astype(BF16)

    def key_tile(k_ref, v_ref, kt, masked, carries):
        k = k_ref[...]
        v = v_ref[...]
        valid = None
        if masked:
            r = lax.broadcasted_iota(jnp.int32, (2 * tile, 1), 0)
            q_pos = jnp.where(r < tile, ta * tile + r, tb * tile + r - tile)
            valid = kt * tile + lax.broadcasted_iota(jnp.int32, (1, tile), 1) < q_pos
        pv = None
        new_carries = []
        for h in range(N_HEADS):
            z2 = _dot_nt(qm_scr[h], k) + bias_ref[h] * LOG2E
            w, carry = _sb_tile(z2, valid, carries[h], cum_ref)
            new_carries.append(carry)
            pv_h = _dot(w.astype(BF16), jnp.where(lane_head == h, v, jnp.zeros_like(v)))
            pv = pv_h if pv is None else pv + pv_h
        return pv, new_carries

    def step(mask1, mask2):
        carries = [carry_scr[h] for h in range(N_HEADS)]
        pv, carries = key_tile(k1_ref, v1_ref, kj, mask1, carries)
        if mask2 is not None:
            pv2, carries = key_tile(k2_ref, v2_ref, kj2, mask2, carries)
            pv = pv + pv2
        for h in range(N_HEADS):
            carry_scr[h] = carries[h]
        acc_scr[...] += pv

    has2 = kj2 >= 0
    single = jnp.logical_not(has2)
    pl.when(jnp.logical_and(has2, kj2 >= ta))(lambda: step(True, True))
    pl.when(jnp.logical_and(has2, jnp.logical_and(kj >= ta, kj2 < ta)))(lambda: step(True, False))
    pl.when(jnp.logical_and(has2, kj < ta))(lambda: step(False, False))
    pl.when(jnp.logical_and(single, kj >= ta))(lambda: step(True, None))
    pl.when(jnp.logical_and(single, kj < ta))(lambda: step(False, None))

    @pl.when(kj2 <= 0)
    def _():
        oa_ref[...] = acc_scr[0:tile, :].astype(BF16)
        ob_ref[...] = acc_scr[tile:2 * tile, :].astype(BF16)


def _cum_matrix(n):
    idx = np.arange(n)
    return jnp.asarray((idx[:, None] >= idx[None, :]).astype(np.float32), BF16)


def _sbattn_prompt(qkv, bias, n_batch, tp):
    tile = ATT_TILE
    nq = tp // tile
    n_pairs = -(-nq // 2)
    ta_list, tb_list, kj_list, pair_list = [], [], [], []
    for p in range(n_pairs):
        ta, tb = 2 * p, min(2 * p + 1, nq - 1)
        for kj in range(tb, -1, -2):
            ta_list.append(ta)
            tb_list.append(tb)
            kj_list.append(kj)
            pair_list.append(p)
    as_arr = lambda v: jnp.asarray(np.array(v, np.int32))
    out_spec = pl.BlockSpec((tile, GROUP_W), lambda b, s, ta, tb, kj, pr, bs: (b * n_pairs + pr[s], 0))
    grid_spec = pltpu.PrefetchScalarGridSpec(
        num_scalar_prefetch=5,
        grid=(n_batch, len(kj_list)),
        in_specs=[pl.BlockSpec((tile, GROUP_W), lambda b, s, ta, tb, kj, pr, bs: (b * nq + ta[s], 0)),
                  pl.BlockSpec((tile, GROUP_W), lambda b, s, ta, tb, kj, pr, bs: (b * nq + tb[s], 0)),
                  pl.BlockSpec((tile, GROUP_W), lambda b, s, ta, tb, kj, pr, bs: (b * nq + kj[s], 1)),
                  pl.BlockSpec((tile, GROUP_W), lambda b, s, ta, tb, kj, pr, bs: (b * nq + kj[s], 2)),
                  pl.BlockSpec((tile, GROUP_W),
                               lambda b, s, ta, tb, kj, pr, bs: (b * nq + jnp.maximum(kj[s] - 1, 0), 1)),
                  pl.BlockSpec((tile, GROUP_W),
                               lambda b, s, ta, tb, kj, pr, bs: (b * nq + jnp.maximum(kj[s] - 1, 0), 2)),
                  pl.BlockSpec((tile, tile), lambda b, s, ta, tb, kj, pr, bs: (0, 0))],
        out_specs=[out_spec, out_spec],
        scratch_shapes=[pltpu.VMEM((N_HEADS, 2 * tile, GROUP_W), BF16),
                        pltpu.VMEM((2 * tile, GROUP_W), F32),
                        pltpu.VMEM((N_HEADS, 2 * tile, 1), F32)],
    )
    o_a, o_b = pl.pallas_call(
        _sbattn_kernel,
        grid_spec=grid_spec,
        out_shape=[jax.ShapeDtypeStruct((n_batch * n_pairs * tile, GROUP_W), BF16)] * 2,
        compiler_params=_params(1, 1),
    )(as_arr(ta_list), as_arr(tb_list), as_arr(kj_list), as_arr(pair_list), bias, qkv, qkv, qkv, qkv, qkv, qkv,
      _cum_matrix(tile))
    o = jnp.stack([o_a.reshape(n_batch, n_pairs, tile, GROUP_W), o_b.reshape(n_batch, n_pairs, tile, GROUP_W)],
                  axis=2)
    return o.reshape(n_batch, 2 * n_pairs * tile, GROUP_W)[:, :tp].reshape(n_batch * tp, GROUP_W)


def _conv_kernel(cb_ref, cc_ref, ch_ref, da_ref, dg_ref, par_ref, cw_ref, oc_ref, od_ref, u_ref, ud_ref,
                 ubuf, dbuf):
    t = pl.program_id(1)
    tile = cb_ref.shape[0]

    @pl.when(t == 0)
    def _():
        ubuf[0:SCONV_HIST, :] = jnp.zeros((SCONV_HIST, GROUP_W), F32)
        dbuf[0:CCONV_HIST, :] = jnp.zeros((CCONV_HIST, GROUP_W), F32)

    u = cc_ref[...] * ch_ref[...]
    ud = da_ref[...] * _sigmoid(dg_ref[...])
    u_ref[...] = u
    ud_ref[...] = ud
    ubuf[SCONV_HIST:SCONV_HIST + tile, :] = u
    dbuf[CCONV_HIST:CCONV_HIST + tile, :] = ud

    conv_c = par_ref[SCONV_W - 1:SCONV_W, :] * u
    for j in range(SCONV_W - 1):
        off = SCONV_HIST - (SCONV_W - 1) + j
        conv_c = conv_c + par_ref[j:j + 1, :] * ubuf[off:off + tile, :]
    oc_ref[...] = (cb_ref[...] * conv_c).astype(BF16)

    acc = cw_ref[CCONV_W - 1:CCONV_W, :] * ud + par_ref[3:4, :]
    for j in range(CCONV_W - 1):
        off = CCONV_HIST - (CCONV_W - 1) + j
        acc = acc + cw_ref[j:j + 1, :] * dbuf[off:off + tile, :]
    y = _layer_norm(acc, par_ref[4:5, :], par_ref[5:6, :])
    od_ref[...] = (y * _sigmoid(y)).astype(BF16)

    ubuf[0:SCONV_HIST, :] = ubuf[tile:tile + SCONV_HIST, :]
    dbuf[0:CCONV_HIST, :] = dbuf[tile:tile + CCONV_HIST, :]


def _conv_prompt(z, par, cw, n_batch, tp):
    tile = ATT_TILE
    nt = tp // tile

    def col(cb):
        return pl.BlockSpec((tile, GROUP_W), lambda b, t: (b * nt + t, cb))

    const = lambda b, t: (0, 0)
    out_spec = pl.BlockSpec((tile, GROUP_W), lambda b, t: (b * nt + t, 0))
    n_rows = n_batch * tp
    return pl.pallas_call(
        _conv_kernel,
        grid=(n_batch, nt),
        in_specs=[col(COL_CB), col(COL_CC), col(COL_CH), col(COL_DA), col(COL_DG),
                  pl.BlockSpec((8, GROUP_W), const), pl.BlockSpec((CCONV_HIST, GROUP_W), const)],
        out_specs=[out_spec, out_spec, out_spec, out_spec],
        out_shape=[jax.ShapeDtypeStruct((n_rows, GROUP_W), BF16), jax.ShapeDtypeStruct((n_rows, GROUP_W), BF16),
                   jax.ShapeDtypeStruct((n_rows, GROUP_W), F32), jax.ShapeDtypeStruct((n_rows, GROUP_W), F32)],
        scratch_shapes=[pltpu.VMEM((SCONV_HIST + tile, GROUP_W), F32),
                        pltpu.VMEM((CCONV_HIST + tile, GROUP_W), F32)],
        compiler_params=_params(1, 1),
    )(z, z, z, z, z, par, cw)


def _eye(n):
    return lax.broadcasted_iota(jnp.int32, (n, n), 0) == lax.broadcasted_iota(jnp.int32, (n, n), 1)


def _row_to_column(r):
    n = r.shape[1]
    return jnp.sum(jnp.where(_eye(n), jnp.broadcast_to(r, (n, n)), 0.0), axis=1, keepdims=True)


def _column_to_row(c):
    n = c.shape[0]
    return jnp.sum(jnp.where(_eye(n), jnp.broadcast_to(c, (n, n)), 0.0), axis=0, keepdims=True)


def _sample_attn_kernel(n_pages, pt_ref, bias_ref, z_ref, cum_ref, cross_ref, *refs):
    k_refs = refs[:n_pages]
    v_refs = refs[n_pages:2 * n_pages]
    o_ref = refs[2 * n_pages]
    page = k_refs[0].shape[2]
    rows = 8
    q = z_ref[:, COL_BQ * GROUP_W:(COL_BQ + 1) * GROUP_W] * (HEAD_DIM ** -0.5)
    q_cols = [jnp.broadcast_to(_row_to_column(q[:, h * HEAD_DIM:(h + 1) * HEAD_DIM]), (HEAD_DIM, page))
              for h in range(N_HEADS)]
    row = lax.broadcasted_iota(jnp.int32, (rows, 1), 0)
    tiles = []
    for p in range(n_pages):
        zp = jnp.zeros((rows, page), F32)
        for h in range(N_HEADS):
            zh = jnp.sum(q_cols[h] * k_refs[p][h], axis=0, keepdims=True) + bias_ref[h]
            zp = jnp.where(row == h, zh, zp)
        tiles.append(zp)
    z = jnp.concatenate(tiles, axis=0)
    sp = _softplus(z)
    hi, lo = _split2(sp)
    cum = _dot(hi, cum_ref[...]) + _dot(lo, cum_ref[...])
    carry = _dot_exact_rhs(cross_ref[...], jnp.broadcast_to(cum[:, 0:1], cum.shape))
    w = jnp.exp(z - cum - carry)
    outs = []
    for h in range(N_HEADS):
        acc = jnp.zeros((HEAD_DIM, page), F32)
        for p in range(n_pages):
            acc = acc + w[p * rows + h:p * rows + h + 1, :] * v_refs[p][h]
        outs.append(_column_to_row(jnp.sum(acc, axis=1, keepdims=True)))
    o_ref[...] = jnp.concatenate(outs, axis=1)


def _sample_attn(z3, cache_kt, cache_vt, page_table, bias, layer):
    n_seq = z3.shape[0]
    n_pages = page_table.shape[1]
    page = cache_kt.shape[4]
    n_cols = z3.shape[2]
    rows = 8
    idx = np.arange(n_pages * rows)
    cross = ((idx[:, None] % rows == idx[None, :] % rows) & (idx[None, :] // rows > idx[:, None] // rows))

    def page_spec(p):
        return pl.BlockSpec((None, None, N_HEADS, HEAD_DIM, page), lambda b, pt, bs: (pt[b, p], layer, 0, 0, 0))

    grid_spec = pltpu.PrefetchScalarGridSpec(
        num_scalar_prefetch=2,
        grid=(n_seq,),
        in_specs=[pl.BlockSpec((None, 1, n_cols), lambda b, pt, bs: (b, 0, 0)),
                  pl.BlockSpec((page, page), lambda b, pt, bs: (0, 0)),
                  pl.BlockSpec((n_pages * rows, n_pages * rows), lambda b, pt, bs: (0, 0))]
                 + [page_spec(p) for p in range(n_pages)] * 2,
        out_specs=pl.BlockSpec((None, 1, GROUP_W), lambda b, pt, bs: (b, 0, 0)),
    )
    out = pl.pallas_call(
        functools.partial(_sample_attn_kernel, n_pages),
        grid_spec=grid_spec,
        out_shape=jax.ShapeDtypeStruct((n_seq, 1, GROUP_W), F32),
        compiler_params=_params(1),
    )(page_table, bias, z3, _cum_matrix(page), jnp.asarray(cross.astype(np.float32), BF16),
      *([cache_kt] * n_pages), *([cache_vt] * n_pages))
    return out.reshape(n_seq, GROUP_W)


def _sample_hgrn_kernel(q_ref, f_ref, i_ref, g_ref, par_ref, s_ref, o_ref, sn_ref):
    logf, key = _hgrn_gates(f_ref[...], par_ref[:, 0:1], par_ref[:, 1:2], par_ref[:, 2:3])
    f = jnp.exp(logf)
    q = q_ref[...]
    v = i_ref[...]
    acc = jnp.zeros(v.shape, F32)
    for k in range(HEAD_DIM):
        s_new = f[k:k + 1, :] * s_ref[k] + key[k:k + 1, :] * v
        sn_ref[k] = s_new
        acc = acc + q[k:k + 1, :] * s_new
    ms = jnp.mean(acc * acc, axis=0, keepdims=True)
    g = g_ref[...]
    o_ref[...] = acc * lax.rsqrt(ms + RMS_EPS) * par_ref[:, 3:4] * (g * _sigmoid(g))


def _sample_hgrn(zt_s, state_t, par_t, layer):
    n_seq = zt_s.shape[1]

    def col(cb):
        return pl.BlockSpec((HEAD_DIM, n_seq), lambda h: (cb * N_HEADS + h, 0))

    return pl.pallas_call(
        _sample_hgrn_kernel,
        grid=(N_HEADS,),
        in_specs=[col(COL_AQ), col(COL_AF), col(COL_AI), col(COL_AG),
                  pl.BlockSpec((HEAD_DIM, 8), lambda h: (h, 0)),
                  pl.BlockSpec((None, None, HEAD_DIM, HEAD_DIM, n_seq), lambda h: (layer, h, 0, 0, 0))],
        out_specs=[pl.BlockSpec((HEAD_DIM, n_seq), lambda h: (h, 0)),
                   pl.BlockSpec((None, HEAD_DIM, HEAD_DIM, n_seq), lambda h: (h, 0, 0, 0))],
        out_shape=[jax.ShapeDtypeStruct((GROUP_W, n_seq), F32),
                   jax.ShapeDtypeStruct((N_HEADS, HEAD_DIM, HEAD_DIM, n_seq), F32)],
        compiler_params=_params(1),
    )(zt_s, zt_s, zt_s, zt_s, par_t, state_t)


def _sample_conv_kernel(z_ref, sc_ref, cc_ref, cpar_ref, cw_ref, oc_ref, od_ref, scn_ref, ccn_ref):
    nb = z_ref.shape[0]

    def zcol(cb):
        return z_ref[:, cb * GROUP_W:(cb + 1) * GROUP_W]

    u = zcol(COL_CC) * zcol(COL_CH)
    cb = zcol(COL_CB)
    ud = zcol(COL_DA) * _sigmoid(zcol(COL_DG))
    for i in range(nb):
        u_i = u[i:i + 1, :]
        conv_c = (cpar_ref[0:1, :] * sc_ref[i, 0:1, :] + cpar_ref[1:2, :] * sc_ref[i, 1:2, :]
                  + cpar_ref[2:3, :] * u_i)
        oc_ref[i:i + 1, :] = cb[i:i + 1, :] * conv_c
        scn_ref[i, 0:1, :] = sc_ref[i, 1:2, :]
        scn_ref[i, 1:2, :] = u_i

        ud_i = ud[i:i + 1, :]
        prev = cc_ref[i]
        conv_d = (jnp.sum(prev * cw_ref[0:CCONV_W - 1, :], axis=0, keepdims=True)
                  + cw_ref[CCONV_W - 1:CCONV_W, :] * ud_i + cpar_ref[3:4, :])
        y = _layer_norm(conv_d, cpar_ref[4:5, :], cpar_ref[5:6, :])
        od_ref[i:i + 1, :] = y * _sigmoid(y)
        ccn_ref[i, 0:CCONV_W - 2, :] = cc_ref[i, 1:CCONV_W - 1, :]
        ccn_ref[i, CCONV_W - 2:CCONV_W - 1, :] = ud_i


def _sample_conv(z_s, state_sconv, state_cconv, cpar, cw, layer):
    n_seq, n_cols = z_s.shape
    nb = SAMPLE_TILE
    const = lambda i: (0, 0)
    row_spec = pl.BlockSpec((nb, GROUP_W), lambda i: (i, 0))
    return pl.pallas_call(
        _sample_conv_kernel,
        grid=(n_seq // nb,),
        in_specs=[pl.BlockSpec((nb, n_cols), lambda i: (i, 0)),
                  pl.BlockSpec((nb, None, SCONV_W - 1, GROUP_W), lambda i: (i, layer, 0, 0)),
                  pl.BlockSpec((nb, None, CCONV_W - 1, GROUP_W), lambda i: (i, layer, 0, 0)),
                  pl.BlockSpec((8, GROUP_W), const),
                  pl.BlockSpec((CCONV_HIST, GROUP_W), const)],
        out_specs=[row_spec, row_spec,
                   pl.BlockSpec((nb, SCONV_W - 1, GROUP_W), lambda i: (i, 0, 0)),
                   pl.BlockSpec((nb, CCONV_W - 1, GROUP_W), lambda i: (i, 0, 0))],
        out_shape=[jax.ShapeDtypeStruct((n_seq, GROUP_W), F32)] * 2
                  + [jax.ShapeDtypeStruct((n_seq, SCONV_W - 1, GROUP_W), F32),
                     jax.ShapeDtypeStruct((n_seq, CCONV_W - 1, GROUP_W), F32)],
        compiler_params=_params(1),
    )(z_s, state_sconv, state_cconv, cpar, cw)


def _tiled_rows(ref, n_rows):
    return jnp.concatenate([ref[pl.ds(c, n_rows, stride=ROW_CHUNKS), :] for c in range(ROW_CHUNKS)], axis=1)


def _store_tiled_rows(ref, y):
    for c in range(ROW_CHUNKS):
        ref[pl.ds(c, y.shape[0], stride=ROW_CHUNKS), :] = y[:, c * 128:(c + 1) * 128]


def _outproj_ln_kernel(alpha, oa_ref, ob_ref, oc_ref, od_ref, w_ref, x_ref, g_ref, b_ref, xo_ref, xt_ref):
    mix = _dot(oa_ref[...], w_ref[0:GROUP_W, :])
    for j, ref in enumerate((ob_ref, oc_ref, od_ref), start=1):
        mix = mix + _dot(ref[...], w_ref[j * GROUP_W:(j + 1) * GROUP_W, :])
    y = _layer_norm(alpha * x_ref[...] + mix, g_ref[...], b_ref[...])
    xo_ref[...] = y
    _store_tiled_rows(xt_ref, y)


def _outproj_ln(oa, ob, oc, od, w, x, g, b, alpha):
    n_rows, d = x.shape
    assert d == ROW_CHUNKS * 128
    tm = _row_tile(n_rows, 1024, 128)
    part = pl.BlockSpec((tm, GROUP_W), lambda i: (i, 0))
    full = pl.BlockSpec((tm, d), lambda i: (i, 0))
    vec = pl.BlockSpec((1, d), lambda i: (0, 0))
    return pl.pallas_call(
        functools.partial(_outproj_ln_kernel, alpha),
        grid=(n_rows // tm,),
        in_specs=[part, part, part, part, pl.BlockSpec((4 * GROUP_W, d), lambda i: (0, 0)), full, vec, vec],
        out_specs=[full, pl.BlockSpec((tm * ROW_CHUNKS, 128), lambda i: (i, 0))],
        out_shape=[jax.ShapeDtypeStruct((n_rows, d), F32), jax.ShapeDtypeStruct((n_rows * ROW_CHUNKS, 128), F32)],
        compiler_params=_params(1),
    )(oa, ob, oc, od, w, x, g, b)


def _router_kernel(x_ref, w1_ref, w2_ref, w3_ref, b_ref, e_ref, g_ref):
    x1, x2, x3 = _split3(x_ref[...])
    w1, w2, w3 = w1_ref[...], w2_ref[...], w3_ref[...]
    logits = (_dot(x1, w1) + (_dot(x1, w2) + _dot(x2, w1)) + (_dot(x1, w3) + _dot(x2, w2) + _dot(x3, w1))
              + b_ref[...])
    n_lanes = logits.shape[1]
    lane = lax.broadcasted_iota(jnp.int32, (1, n_lanes), 1)
    e_out = jnp.zeros(logits.shape, jnp.int32)
    g_out = jnp.zeros(logits.shape, F32)
    denom = jnp.zeros((logits.shape[0], 1), F32)
    top = None
    for k in range(TOP_K):
        m = jnp.max(logits, axis=-1, keepdims=True)
        idx = jnp.min(jnp.where(logits == m, lane, n_lanes), axis=-1, keepdims=True)
        if top is None:
            top = m
        p = jnp.exp(m - top)
        denom = denom + p
        e_out = jnp.where(lane == k, idx, e_out)
        g_out = jnp.where(lane == k, p, g_out)
        logits = jnp.where(lane == idx, NEG_BIG * 2.0, logits)
    e_ref[...] = e_out
    g_ref[...] = g_out / denom


def _router(x, w_router, b_router):
    n_rows, d = x.shape
    n_exp = w_router.shape[1]
    lanes = 128
    wp = jnp.zeros((d, lanes), F32).at[:, :n_exp].set(w_router)
    bp = jnp.full((1, lanes), NEG_BIG, F32).at[0, :n_exp].set(b_router)
    w1, w2, w3 = _split3(wp)
    tm = _row_tile(n_rows, 1024, 128)
    wspec = pl.BlockSpec((d, lanes), lambda i: (0, 0))
    ospec = pl.BlockSpec((tm, lanes), lambda i: (i, 0))
    return pl.pallas_call(
        _router_kernel,
        grid=(n_rows // tm,),
        in_specs=[pl.BlockSpec((tm, d), lambda i: (i, 0)), wspec, wspec, wspec,
                  pl.BlockSpec((1, lanes), lambda i: (0, 0))],
        out_specs=[ospec, ospec],
        out_shape=[jax.ShapeDtypeStruct((n_rows, lanes), jnp.int32), jax.ShapeDtypeStruct((n_rows, lanes), F32)],
        compiler_params=_params(1),
    )(x, w1, w2, w3, bp)


def _expert_kernel(be_ref, nu_ref, idx_ref, x_hbm, wgu_ref, bgu_ref, wd_ref, bd_ref, y_hbm,
                   idx_smem, xbuf, ybuf, wgu_bf, wd_bf, sem_idx, sem_g, sem_s):
    i = pl.program_id(0)
    n = pl.num_programs(0)
    blk = xbuf.shape[1] // ROW_CHUNKS
    d_exp = wd_ref.shape[0]

    def idx_copy(block, s):
        return pltpu.make_async_copy(idx_ref.at[block], idx_smem.at[s], sem_idx.at[s])

    def gather_copy(s, j, r):
        return pltpu.make_async_copy(x_hbm.at[pl.ds(pl.multiple_of(r, ROW_CHUNKS), ROW_CHUNKS)],
                                     xbuf.at[s, pl.ds(j * ROW_CHUNKS, ROW_CHUNKS)], sem_g.at[s])

    def scatter_copy(s, j, r):
        return pltpu.make_async_copy(ybuf.at[s, pl.ds(j * ROW_CHUNKS, ROW_CHUNKS)],
                                     y_hbm.at[pl.ds(pl.multiple_of(r, ROW_CHUNKS), ROW_CHUNKS)], sem_s.at[s])

    def start_gather(s):
        for j in range(blk):
            gather_copy(s, j, idx_smem[s, j]).start(priority=j % DMA_THREADS)

    def start_scatter(s):
        for j in range(blk):
            scatter_copy(s, j, idx_smem[s, blk + j]).start(priority=j % DMA_THREADS)

    def wait_rows(copy, s):
        for j in range(blk):
            copy(s, j, 0).wait()

    @pl.when(i == 0)
    def _():
        idx_copy(0, 0).start()
        idx_copy(0, 0).wait()
        idx_copy(1, 1).start()
        start_gather(0)

    def step(s):
        o = 1 - s
        wait_rows(gather_copy, s)
        idx_copy(i + 1, o).wait()

        @pl.when(i >= 2)
        def _():
            wait_rows(scatter_copy, s)

        changed = jnp.logical_or(i == 0, be_ref[i] != be_ref[jnp.maximum(i - 1, 0)])

        @pl.when(changed)
        def _():
            wgu_bf[...] = wgu_ref[...].astype(BF16)
            wd_bf[...] = wd_ref[...].astype(BF16)

        start_gather(o)

        @pl.when(i < nu_ref[0])
        def _():
            xb = _tiled_rows(xbuf.at[s], blk).astype(BF16)
            gu = _dot(xb, wgu_bf[...]) + bgu_ref[...]
            gate = jnp.minimum(gu[:, :d_exp], SWIGLU_LIMIT)
            up = jnp.clip(gu[:, d_exp:], -SWIGLU_LIMIT, SWIGLU_LIMIT)
            act = (up + 1.0) * gate * _sigmoid(SWIGLU_ALPHA * gate)
            _store_tiled_rows(ybuf.at[s], _dot(act.astype(BF16), wd_bf[...]) + bd_ref[...])

        @pl.when(i >= nu_ref[0])
        def _():
            ybuf[s] = jnp.zeros(ybuf.shape[1:], F32)

        start_scatter(s)
        idx_copy(i + 2, s).start()

        @pl.when(i == n - 1)
        def _():
            wait_rows(scatter_copy, s)
            wait_rows(gather_copy, o)
            idx_copy(i + 2, s).wait()

            @pl.when(n >= 2)
            def _():
                wait_rows(scatter_copy, o)

    pl.when(i % 2 == 0)(lambda: step(0))
    pl.when(i % 2 == 1)(lambda: step(1))


def _experts(xt, idx, block_e, n_used, w_gate_up, b_gate_up, w_down, b_down, layer, n_slots):
    n_blocks = idx.shape[0] - 2
    blk = idx.shape[1] // 2
    d = w_gate_up.shape[2]
    d2 = w_gate_up.shape[3]
    grid_spec = pltpu.PrefetchScalarGridSpec(
        num_scalar_prefetch=2,
        grid=(n_blocks,),
        in_specs=[pl.BlockSpec(idx.shape, lambda i, be, nu: (0, 0)),
                  pl.BlockSpec(memory_space=pl.ANY),
                  pl.BlockSpec((None, None, d, d2), lambda i, be, nu: (layer, be[i], 0, 0)),
                  pl.BlockSpec((None, None, 1, d2), lambda i, be, nu: (layer, be[i], 0, 0)),
                  pl.BlockSpec((None, None, d2 // 2, d), lambda i, be, nu: (layer, be[i], 0, 0)),
                  pl.BlockSpec((None, None, 1, d), lambda i, be, nu: (layer, be[i], 0, 0))],
        out_specs=pl.BlockSpec(memory_space=pl.ANY),
        scratch_shapes=[pltpu.SMEM((2, 2 * blk), jnp.int32),
                        pltpu.VMEM((2, blk * ROW_CHUNKS, 128), F32),
                        pltpu.VMEM((2, blk * ROW_CHUNKS, 128), F32),
                        pltpu.VMEM((d, d2), BF16),
                        pltpu.VMEM((d2 // 2, d), BF16),
                        pltpu.SemaphoreType.DMA((2,)),
                        pltpu.SemaphoreType.DMA((2,)),
                        pltpu.SemaphoreType.DMA((2,))],
    )
    return pl.pallas_call(
        _expert_kernel,
        grid_spec=grid_spec,
        out_shape=jax.ShapeDtypeStruct((n_slots * ROW_CHUNKS, 128), F32),
        compiler_params=_params(0, 1),
    )(block_e, n_used, idx, xt, w_gate_up, b_gate_up, w_down, b_down)


def _route(top_e, n_rows, n_experts):
    blk = MOE_BLOCK
    n_assign = n_rows * TOP_K
    n_blocks = -(-n_assign // blk) + n_experts
    flat_e = top_e.reshape(-1)
    a_bits = n_assign.bit_length()
    assert (n_experts << a_bits) < 2 ** 31
    packed = jnp.sort(flat_e * (1 << a_bits) + jnp.arange(n_assign, dtype=jnp.int32))
    order = packed & ((1 << a_bits) - 1)
    sizes = jnp.sum(flat_e[:, None] == jnp.arange(n_experts, dtype=jnp.int32)[None, :], axis=0, dtype=jnp.int32)
    blocks_e = (sizes + blk - 1) // blk
    blk_end = jnp.cumsum(blocks_e)
    blk_start = blk_end - blocks_e
    grp_start = jnp.cumsum(sizes) - sizes
    block = jnp.arange(n_blocks, dtype=jnp.int32)
    block_e = jnp.minimum(jnp.sum(blk_end[None, :] <= block[:, None], axis=1, dtype=jnp.int32), n_experts - 1)
    lane = jnp.arange(blk, dtype=jnp.int32)[None, :]
    off = (block - blk_start[block_e])[:, None] * blk + lane
    valid = off < sizes[block_e][:, None]
    a = order[jnp.where(valid, grp_start[block_e][:, None] + off, 0)]
    tok = a // TOP_K
    src = jnp.where(valid, tok, 0)
    dump = n_assign + (block % 2)[:, None] * blk + lane
    dst = jnp.where(valid, (a % TOP_K) * n_rows + tok, dump)
    idx = jnp.concatenate([src, dst], axis=1) * ROW_CHUNKS
    idx = jnp.concatenate([idx, jnp.zeros((2, 2 * blk), jnp.int32)], axis=0)
    n_used = blk_end[-1:].astype(jnp.int32)
    return idx, block_e, n_used, n_assign + 2 * blk


def _combine_ln_kernel(alpha, x_ref, y0_ref, y1_ref, y2_ref, y3_ref, gate_ref, g_ref, b_ref, xo_ref, xb_ref):
    gates = gate_ref[...]
    tm = x_ref.shape[0]
    mix = gates[:, 0:1] * _tiled_rows(y0_ref, tm)
    for k, ref in enumerate((y1_ref, y2_ref, y3_ref), start=1):
        mix = mix + gates[:, k:k + 1] * _tiled_rows(ref, tm)
    y = _layer_norm(alpha * x_ref[...] + mix, g_ref[...], b_ref[...])
    xo_ref[...] = y
    xb_ref[...] = y.astype(BF16)


def _combine_ln(x, y_slots, gates, g, b, alpha):
    n_rows, d = x.shape
    tm = _row_tile(n_rows, 512, 64)
    nt = n_rows // tm
    full = pl.BlockSpec((tm, d), lambda i: (i, 0))
    vec = pl.BlockSpec((1, d), lambda i: (0, 0))

    def yspec(k):
        return pl.BlockSpec((tm * ROW_CHUNKS, 128), lambda i: (k * nt + i, 0))

    return pl.pallas_call(
        functools.partial(_combine_ln_kernel, alpha),
        grid=(nt,),
        in_specs=[full, yspec(0), yspec(1), yspec(2), yspec(3),
                  pl.BlockSpec((tm, gates.shape[1]), lambda i: (i, 0)), vec, vec],
        out_specs=[full, full],
        out_shape=[jax.ShapeDtypeStruct((n_rows, d), F32), jax.ShapeDtypeStruct((n_rows, d), BF16)],
        compiler_params=_params(1),
    )(x, y_slots, y_slots, y_slots, y_slots, gates, g, b)


def _pad_rows(a, n):
    return jnp.zeros((n, a.shape[1]), a.dtype).at[:a.shape[0]].set(a)


def kernel(x_prompt, x_sample, cache_sb_k, cache_sb_v, page_table, state_hgrn, state_sconv, state_cconv, meta_tokens, w_in, w_out, sb_bias, hg_lb_logits, hg_norm_w, sconv_w, cconv_w, cconv_b, cconv_ln_g, cconv_ln_b, ln1_g, ln1_b, w_router, b_router, w_gate_up, b_gate_up, w_down, b_down, ln2_g, ln2_b):
    n_batch, seq, d = x_prompt.shape
    n_seq = x_sample.shape[0]
    depth = w_in.shape[0]
    n_meta = meta_tokens.shape[0]
    n_experts = w_router.shape[2]
    t_len = n_meta + seq
    tp = -(-t_len // ATT_TILE) * ATT_TILE
    n_prompt_rows = n_batch * tp
    n_rows = n_prompt_rows + n_seq
    alpha = float((2 * depth) ** 0.25)
    assert d == 4 * GROUP_W and x_sample.shape[1] == 1 and n_seq % SAMPLE_TILE == 0

    pieces = []
    for b in range(n_batch):
        pieces += [meta_tokens.astype(F32), x_prompt[b], jnp.zeros((tp - t_len, d), F32)]
    pieces.append(x_sample.reshape(n_seq, d))
    x = jnp.concatenate(pieces, axis=0)
    xb = x.astype(BF16)

    lb_cum = jnp.cumsum(jax.nn.softmax(hg_lb_logits.astype(F32), axis=0), axis=0)
    lb_all = lb_cum - lb_cum[0]

    cache_kt = jnp.transpose(cache_sb_k, (0, 1, 3, 4, 2))
    cache_vt = jnp.transpose(cache_sb_v, (0, 1, 3, 4, 2))
    state_t = jnp.transpose(state_hgrn, (1, 2, 3, 4, 0))

    hg_p, hg_s, kp_l, vp_l, ks_l, vs_l, sc_p, sc_s, cc_p, cc_s = ([] for _ in range(10))
    for l in range(depth):
        lb = lb_all[l]
        hpar = _pad_rows(jnp.stack([jnp.log(lb), jnp.log1p(-lb), 1.0 - lb, hg_norm_w[l]]), 8)
        cpar = _pad_rows(jnp.concatenate([sconv_w[l], cconv_b[l][None], cconv_ln_g[l][None],
                                          cconv_ln_b[l][None]], axis=0), 8)
        cw = _pad_rows(cconv_w[l], CCONV_HIST)

        z, qkv = _inproj(xb, w_in[l].astype(BF16))

        oa_p, st = _hgrn_prompt(z, hpar, n_batch, tp, t_len)
        ob_p = _sbattn_prompt(qkv, sb_bias[l], n_batch, tp)
        oc_p, od_p, u_p, ud_p = _conv_prompt(z, cpar, cw, n_batch, tp)

        z_s = z[n_prompt_rows:]
        ob_s = _sample_attn(z_s.reshape(n_seq, 1, z.shape[1]), cache_kt, cache_vt, page_table, sb_bias[l], l)
        oa_t, s_new = _sample_hgrn(z_s.T, state_t, hpar.T, l)
        oa_s = oa_t.T
        oc_s, od_s, sc_new, cc_new = _sample_conv(z_s, state_sconv, state_cconv, cpar, cw, l)

        oa = jnp.concatenate([oa_p, oa_s.astype(BF16)], axis=0)
        ob = jnp.concatenate([ob_p, ob_s.astype(BF16)], axis=0)
        oc = jnp.concatenate([oc_p, oc_s.astype(BF16)], axis=0)
        od = jnp.concatenate([od_p, od_s.astype(BF16)], axis=0)
        x, xt = _outproj_ln(oa, ob, oc, od, w_out[l].astype(BF16), x, ln1_g[l][None], ln1_b[l][None], alpha)

        top_e, gates = _router(x, w_router[l], b_router[l])
        idx, block_e, n_used, n_slots = _route(top_e[:, :TOP_K], n_rows, n_experts)
        y_slots = _experts(xt, idx, block_e, n_used, w_gate_up, b_gate_up.reshape(depth, n_experts, 1, -1),
                           w_down, b_down.reshape(depth, n_experts, 1, -1), l, n_slots)
        x, xb = _combine_ln(x, y_slots, gates, ln2_g[l][None], ln2_b[l][None], alpha)

        def prompt_heads(cb):
            cols = slice(cb * GROUP_W, (cb + 1) * GROUP_W)
            return jnp.stack([z[b * tp:b * tp + t_len, cols] for b in range(n_batch)], axis=0).reshape(
                n_batch, t_len, N_HEADS, HEAD_DIM)

        st4 = st.reshape(n_batch, N_HEADS, HEAD_DIM, N_HEADS, HEAD_DIM)
        hg_p.append(jnp.stack([st4[:, h, :, h, :] for h in range(N_HEADS)], axis=1).swapaxes(-1, -2))
        hg_s.append(s_new)
        kp_l.append(prompt_heads(COL_BK))
        vp_l.append(prompt_heads(COL_BV))
        ks_l.append(z_s[:, COL_BK * GROUP_W:(COL_BK + 1) * GROUP_W].reshape(n_seq, 1, N_HEADS, HEAD_DIM))
        vs_l.append(z_s[:, COL_BV * GROUP_W:(COL_BV + 1) * GROUP_W].reshape(n_seq, 1, N_HEADS, HEAD_DIM))
        sc_p.append(u_p.reshape(n_batch, tp, GROUP_W)[:, t_len - (SCONV_W - 1):t_len])
        cc_p.append(ud_p.reshape(n_batch, tp, GROUP_W)[:, t_len - (CCONV_W - 1):t_len])
        sc_s.append(sc_new)
        cc_s.append(cc_new)

    y_prompt = x[:n_prompt_rows].reshape(n_batch, tp, d)[:, n_meta:t_len]
    y_sample = x[n_prompt_rows:].reshape(n_seq, 1, d)
    return (y_prompt, y_sample,
            jnp.stack(hg_p, axis=1), jnp.transpose(jnp.stack(hg_s, axis=0), (4, 0, 1, 2, 3)),
            jnp.stack(kp_l, axis=1), jnp.stack(vp_l, axis=1),
            jnp.stack(ks_l, axis=1), jnp.stack(vs_l, axis=1),
            jnp.stack(sc_p, axis=1), jnp.stack(sc_s, axis=1),
            jnp.stack(cc_p, axis=1), jnp.stack(cc_s, axis=1))
```
